```python
import jax, jax.numpy as jnp
from jax import lax
import numpy as np

D_MODEL = 2048
BATCH = 16
SEQ = 2048
DEPTH = 4

HEAD_DIM = 64
ROPE_THETA = 10000.0
EPS = 1e-6
BLOCK_Q = 128
NEG = -1e30
D_FF = 4 * D_MODEL
A_HEADS = D_MODEL // 256
A_WIDTH = A_HEADS * HEAD_DIM
A_Q_RANK = 3 * D_MODEL // 16
IDX_HEADS = 16
IDX_DIM = HEAD_DIM
TOPK_MAX = 256
B_HEADS = 3 * D_MODEL // 512
B_WIDTH = B_HEADS * HEAD_DIM
DILATED_PATTERNS = ((128, 1), (512, 4), (2048, 16))
POOL_WINDOWS = (2, 4, 8, 16)
C_WIDTH = 3 * D_MODEL // 8
POOL_GROUPS = len(POOL_WINDOWS)
POOL_GROUP_DIM = C_WIDTH // POOL_GROUPS
IN_SIZES = (A_Q_RANK, HEAD_DIM, HEAD_DIM, IDX_DIM, IDX_HEADS, B_WIDTH, B_WIDTH, B_WIDTH, C_WIDTH)
IN_COLS = sum(IN_SIZES)
IN_SPLITS = tuple(int(v) for v in np.cumsum(IN_SIZES)[:-1])
MIX_WIDTH = A_WIDTH + B_WIDTH + C_WIDTH

kernel_name = "hybrid_dsa_dilated_pool_trunk"


def rms_norm(x, g):
    x32 = x.astype(jnp.float32)
    y = x32 * lax.rsqrt(jnp.mean(x32 * x32, axis=-1, keepdims=True) + EPS)
    return (y * g.astype(jnp.float32)).astype(x.dtype)


def rope(x, pos):
    half = x.shape[-1] // 2
    inv = ROPE_THETA ** (-jnp.arange(half, dtype=jnp.float32) / half)
    ang = pos.astype(jnp.float32)[:, None] * inv[None, :]
    cos = jnp.cos(ang)[:, None, :].astype(x.dtype)
    sin = jnp.sin(ang)[:, None, :].astype(x.dtype)
    x1, x2 = x[..., :half], x[..., half:]
    return jnp.concatenate([x1 * cos - x2 * sin, x2 * cos + x1 * sin], axis=-1)


def dsa_attention(q, k, v, q_idx, k_idx, w_idx):
    B, S, HA, D = q.shape
    topk = min(TOPK_MAX, S // 4)
    nblk = S // BLOCK_Q
    key_pos = jnp.arange(S)
    k_idx32 = k_idx.astype(jnp.float32)

    def one_block(n):
        t0 = n * BLOCK_Q
        qb = lax.dynamic_slice_in_dim(q, t0, BLOCK_Q, axis=1)
        qib = lax.dynamic_slice_in_dim(q_idx, t0, BLOCK_Q, axis=1)
        wb = lax.dynamic_slice_in_dim(w_idx, t0, BLOCK_Q, axis=1)
        qpos = t0 + jnp.arange(BLOCK_Q)
        dots = jnp.einsum('bqhd,bsd->bqhs', qib.astype(jnp.float32), k_idx32) * (IDX_DIM ** -0.5)
        score = jnp.einsum('bqhs,bqh->bqs', jax.nn.relu(dots), wb.astype(jnp.float32))
        admissible = key_pos[None, :] <= qpos[:, None]
        score = jnp.where(admissible[None], score, NEG)
        _, sel = lax.top_k(score, topk)
        valid = sel <= qpos[None, :, None]
        ks = jax.vmap(lambda kb, ib: kb[ib])(k, sel)
        vs = jax.vmap(lambda vb, ib: vb[ib])(v, sel)
        logits = jnp.einsum('bqhd,bqkd->bqhk', qb.astype(jnp.float32), ks.astype(jnp.float32)) * (D ** -0.5)
        logits = jnp.where(valid[:, :, None, :], logits, NEG)
        p = jax.nn.softmax(logits, axis=-1)
        return jnp.einsum('bqhk,bqkd->bqhd', p, vs.astype(jnp.float32)).astype(q.dtype)

    out = lax.map(one_block, jnp.arange(nblk))
    return jnp.moveaxis(out, 0, 1).reshape(B, S, HA, D)


def dilated_branch(q, k, v, window, dilation):
    B, S, H, D = q.shape
    span = window // dilation
    ld = S // dilation
    blk = min(BLOCK_Q, ld)
    nb = -(-ld // blk)
    lp = nb * blk

    def to_sub(a):
        a = a.reshape(B, ld, dilation, H, D).transpose(0, 2, 3, 1, 4)
        a = jnp.pad(a, ((0, 0), (0, 0), (0, 0), (0, lp - ld), (0, 0)))
        return a.reshape(B, dilation, H, nb, blk, D).astype(jnp.float32)

    def with_prev(a):
        prev = jnp.pad(a[:, :, :, :-1], ((0, 0), (0, 0), (0, 0), (1, 0), (0, 0), (0, 0)))
        return jnp.concatenate([prev, a], axis=4)

    qs = to_sub(q)
    kk = with_prev(to_sub(k))
    vv = with_prev(to_sub(v))
    s = jnp.einsum('brhnqd,brhnkd->brhnqk', qs, kk) * (D ** -0.5)
    qi = jnp.arange(blk)[:, None]
    ki = jnp.arange(2 * blk)[None, :]
    dist = blk + qi - ki
    in_band = (dist >= 0) & (dist <= span)
    has_prev = (jnp.arange(nb)[:, None, None] > 0) | (ki >= blk)[None]
    valid = in_band[None] & has_prev
    s = jnp.where(valid, s, NEG)
    m = jnp.max(s, axis=-1, keepdims=True)
    p = jnp.exp(s - m)
    l = jnp.sum(p, axis=-1, keepdims=True)
    o = jnp.einsum('brhnqk,brhnkd->brhnqd', p, vv) / l
    lse = (m + jnp.log(l))[..., 0]
    o = o.reshape(B, dilation, H, lp, D)[:, :, :, :ld].transpose(0, 3, 1, 2, 4).reshape(B, S, H, D)
    lse = lse.reshape(B, dilation, H, lp)[..., :ld].transpose(0, 3, 1, 2).reshape(B, S, H)
    return o, lse


def dilated_attention(q, k, v):
    outs, lses = [], []
    for window, dilation in DILATED_PATTERNS:
        o, lse = dilated_branch(q, k, v, window, dilation)
        outs.append(o)
        lses.append(lse)
    wts = jax.nn.softmax(jnp.stack(lses, axis=0), axis=0)
    out = jnp.einsum('pbsh,pbshd->bshd', wts, jnp.stack(outs, axis=0))
    return out.astype(q.dtype)


def pool_mixer(u, w_pool, scale):
    B, S, C = u.shape
    u32 = u.astype(jnp.float32).reshape(B, S, POOL_GROUPS, POOL_GROUP_DIM)
    csum = jnp.pad(jnp.cumsum(u32, axis=1), ((0, 0), (1, 0), (0, 0), (0, 0)))
    pos = jnp.arange(1, S + 1, dtype=jnp.float32)
    means = []
    for g, w in enumerate(POOL_WINDOWS):
        cg = csum[:, :, g]
        lower = jnp.pad(cg, ((0, 0), (w - 1, 0), (0, 0)))[:, :S]
        count = jnp.minimum(pos, float(w))[None, :, None]
        means.append((cg[:, 1:] - lower) / count)
    y = jnp.stack(means, axis=2) - u32
    y = jnp.einsum('bsgc,gcd->bsgd', y, w_pool.astype(jnp.float32))
    return (y.reshape(B, S, C) * scale.astype(jnp.float32)).astype(u.dtype)


def setup_inputs(seed: int = 0) -> dict:
    key = jax.random.key(seed)
    ks = jax.random.split(key, 13)
    f32 = jnp.float32

    def normal(k, shape, fan_in):
        return jax.random.normal(k, shape, f32) * (fan_in ** -0.5)

    def gain(k, shape):
        return 1.0 + 0.02 * jax.random.normal(k, shape, f32)

    return {
        "x": jax.random.normal(ks[0], (BATCH, SEQ, D_MODEL), f32),
        "g_mix": gain(ks[1], (DEPTH, D_MODEL)),
        "w_in": normal(ks[2], (DEPTH, D_MODEL, IN_COLS), D_MODEL),
        "g_cq": gain(ks[3], (DEPTH, A_Q_RANK)),
        "w_uq": normal(ks[4], (DEPTH, A_Q_RANK, A_WIDTH), A_Q_RANK),
        "w_uq_idx": normal(ks[5], (DEPTH, A_Q_RANK, IDX_HEADS * IDX_DIM), A_Q_RANK),
        "w_pool": normal(ks[6], (DEPTH, POOL_GROUPS, POOL_GROUP_DIM, POOL_GROUP_DIM), POOL_GROUP_DIM),
        "pool_scale": 1.0 + 0.1 * jax.random.normal(ks[7], (DEPTH, C_WIDTH), f32),
        "w_o": normal(ks[8], (DEPTH, MIX_WIDTH, D_MODEL), MIX_WIDTH),
        "g_mlp": gain(ks[9], (DEPTH, D_MODEL)),
        "w_up": normal(ks[10], (DEPTH, D_MODEL, D_FF), D_MODEL),
        "w_down": normal(ks[11], (DEPTH, D_FF, D_MODEL), D_FF),
        "g_final": gain(ks[12], (D_MODEL,)),
    }


def reference(x, g_mix, w_in, g_cq, w_uq, w_uq_idx, w_pool, pool_scale, w_o, g_mlp, w_up, w_down, g_final):
    B, S, _ = x.shape
    pos = jnp.arange(S)
    for l in range(DEPTH):
        h = rms_norm(x, g_mix[l])
        z = h @ w_in[l]
        c_q, k_a, v_a, k_i, w_i, q_b, k_b, v_b, u_c = jnp.split(z, IN_SPLITS, axis=-1)
        c_qn = rms_norm(c_q, g_cq[l])
        q_a = rope((c_qn @ w_uq[l]).reshape(B, S, A_HEADS, HEAD_DIM), pos)
        q_i = rope((c_qn @ w_uq_idx[l]).reshape(B, S, IDX_HEADS, IDX_DIM), pos)
        k_a = rope(k_a[:, :, None, :], pos)[:, :, 0]
        k_i = rope(k_i[:, :, None, :], pos)[:, :, 0]
        o_a = dsa_attention(q_a, k_a, v_a, q_i, k_i, w_i * (IDX_HEADS ** -0.5))
        q_b = rope(q_b.reshape(B, S, B_HEADS, HEAD_DIM), pos)
        k_b = rope(k_b.reshape(B, S, B_HEADS, HEAD_DIM), pos)
        v_b = v_b.reshape(B, S, B_HEADS, HEAD_DIM)
        o_b = dilated_attention(q_b, k_b, v_b)
        o_c = pool_mixer(u_c, w_pool[l], pool_scale[l])
        mix = jnp.concatenate([o_a.reshape(B, S, A_WIDTH), o_b.reshape(B, S, B_WIDTH), o_c], axis=-1)
        x = x + mix @ w_o[l]
        h = rms_norm(x, g_mlp[l])
        x = x + jnp.square(jax.nn.relu(h @ w_up[l])) @ w_down[l]
    return rms_norm(x, g_final)
```

```python
import functools

import numpy as np
import jax
import jax.numpy as jnp
from jax import lax
from jax.experimental import pallas as pl
from jax.experimental.pallas import tpu as pltpu

F32 = jnp.float32
BF16 = jnp.bfloat16

LANES = 128
VMEM_LIMIT = 56 * 1024 * 1024

HEAD_DIM = 64
HALF = HEAD_DIM // 2
ROPE_THETA = 10000.0
EPS = 1e-6
NEG = -1e30
INT_MIN = -(2 ** 31)
A_HEADS = 8
IDX_HEADS = 16
TOPK_MAX = 256
B_HEADS = 12
B_PAIRS = B_HEADS // 2
DILATED_PATTERNS = ((128, 1), (512, 4), (2048, 16))
POOL_WINDOWS = (2, 4, 8, 16)
POOL_HIST = 16

TQ = 256
TK = 256
_NT = (((1,), (1,)), ((), ()))


def _params(sem):
    return pltpu.CompilerParams(dimension_semantics=sem, vmem_limit_bytes=VMEM_LIMIT)


def _resident(shape):
    return pl.BlockSpec(shape, lambda *_: (0,) * len(shape), pipeline_mode=pl.Buffered(1))


def _padded_head_cols(base):
    cols = np.full((LANES,), -1, np.int64)
    cols[0:HALF] = base + np.arange(HALF)
    cols[2 * HALF:3 * HALF] = base + HALF + np.arange(HALF)
    return cols


def _pair_head_cols(base):
    h0, h1 = base, base + HEAD_DIM
    return np.concatenate([h0 + np.arange(HALF), h1 + np.arange(HALF),
                           h0 + HALF + np.arange(HALF), h1 + HALF + np.arange(HALF)])


def _take_cols(w, cols):
    pieces, start = [], 0
    for end in range(1, len(cols) + 1):
        run_ends = end == len(cols) or (cols[end] != cols[end - 1] + 1 if cols[end - 1] >= 0
                                        else cols[end] >= 0)
        if run_ends:
            c0 = int(cols[start])
            pieces.append(jnp.zeros((w.shape[0], end - start), w.dtype) if c0 < 0
                          else w[:, c0:c0 + end - start])
            start = end
    return jnp.concatenate(pieces, axis=1)


def _in_layout(d_model):
    a_q_rank = 3 * d_model // 16
    b_width = B_HEADS * HEAD_DIM
    c_width = 3 * d_model // 8
    sizes = (a_q_rank, HEAD_DIM, HEAD_DIM, HEAD_DIM, IDX_HEADS, b_width, b_width, b_width, c_width)
    off = np.concatenate([[0], np.cumsum(sizes)])
    o_cq, o_ka, o_va, o_ki, o_wi, o_qb, o_kb, o_vb, o_uc = (int(v) for v in off[:-1])
    pad = np.full((LANES,), -1, np.int64)
    va = pad.copy()
    va[:HEAD_DIM] = o_va + np.arange(HEAD_DIM)
    wi = pad.copy()
    wi[:IDX_HEADS] = o_wi + np.arange(IDX_HEADS)
    cols = np.concatenate(
        [o_cq + np.arange(a_q_rank),
         _padded_head_cols(o_ka), _padded_head_cols(o_ki), va, wi]
        + [_pair_head_cols(o_qb + p * LANES) for p in range(B_PAIRS)]
        + [_pair_head_cols(o_kb + p * LANES) for p in range(B_PAIRS)]
        + [o_vb + np.arange(b_width), o_uc + np.arange(c_width)])
    scale = np.ones((cols.shape[0],), np.float32)
    wi_start = a_q_rank + 3 * LANES
    scale[wi_start:wi_start + LANES] = (IDX_HEADS ** -0.5) * (HEAD_DIM ** -0.5)
    qb_start = a_q_rank + 4 * LANES
    scale[qb_start:qb_start + b_width] = HEAD_DIM ** -0.5
    return cols, scale, a_q_rank, b_width, c_width


def _rope_tables(seq):
    inv = ROPE_THETA ** (-jnp.arange(HALF, dtype=F32) / HALF)
    ang = jnp.arange(seq, dtype=F32)[:, None] * inv[None, :]
    cos, sin = jnp.cos(ang), jnp.sin(ang)
    return (jnp.concatenate([cos, cos, cos, cos], axis=1),
            jnp.concatenate([-sin, -sin, sin, sin], axis=1))


def _dilated_bias(nkb):
    d = np.arange(nkb)[:, None, None]
    delta = d * TK + np.arange(TQ)[None, None, :] - np.arange(TK)[None, :, None]
    count = np.zeros(delta.shape, np.float64)
    for window, dil in DILATED_PATTERNS:
        count += (delta >= 0) & (delta % dil == 0) & (delta <= window)
    with np.errstate(divide="ignore"):
        bias = np.where(count > 0, np.log(np.maximum(count, 1.0)), NEG)
    return jnp.asarray(bias, F32)


def _rope_store(dst_ref, x, cos, sin_signed):
    for j in range(x.shape[1] // LANES):
        xb = x[:, j * LANES:(j + 1) * LANES]
        dst_ref[:, j * LANES:(j + 1) * LANES] = (
            xb * cos + pltpu.roll(xb, 2 * HALF, axis=1) * sin_signed).astype(dst_ref.dtype)


def _in_proj_kernel(x_ref, g_ref, w_ref, gcq_ref, wq_ref, cos_ref, sin_ref,
                    qa_ref, qi_ref, kvw_ref, qb_ref, kb_ref, vb_ref, uc_ref,
                    *, a_q_rank, b_width):
    x = x_ref[...]
    h = (x * lax.rsqrt(jnp.mean(x * x, axis=-1, keepdims=True) + EPS) * g_ref[...]).astype(BF16)
    cos, sin_signed = cos_ref[...], sin_ref[...]

    def proj(start, width):
        return jnp.dot(h, w_ref[:, start:start + width], preferred_element_type=F32)

    cq = proj(0, a_q_rank)
    cqn = (cq * lax.rsqrt(jnp.mean(cq * cq, axis=-1, keepdims=True) + EPS) * gcq_ref[...]).astype(BF16)
    qa_w = A_HEADS * LANES
    _rope_store(qa_ref, jnp.dot(cqn, wq_ref[:, :qa_w], preferred_element_type=F32), cos, sin_signed)
    _rope_store(qi_ref, jnp.dot(cqn, wq_ref[:, qa_w:], preferred_element_type=F32), cos, sin_signed)

    kvw = proj(a_q_rank, 4 * LANES)
    _rope_store(kvw_ref.at[:, :2 * LANES], kvw[:, :2 * LANES], cos, sin_signed)
    kvw_ref[:, 2 * LANES:] = kvw[:, 2 * LANES:].astype(BF16)
    start = a_q_rank + 4 * LANES
    _rope_store(qb_ref, proj(start, b_width), cos, sin_signed)
    _rope_store(kb_ref, proj(start + b_width, b_width), cos, sin_signed)
    vb_ref[...] = proj(start + 2 * b_width, b_width).astype(BF16)
    uc_ref[...] = proj(start + 3 * b_width, uc_ref.shape[1]).astype(BF16)


def _in_proj(x2, g, w, gcq, wq, cos, sin_signed, *, seq, tm, a_q_rank, b_width, c_width):
    n, d = x2.shape
    nseq = seq // tm
    row = lambda width: pl.BlockSpec((tm, width), lambda i: (i, 0))
    tab = pl.BlockSpec((tm, LANES), lambda i: (i % nseq, 0))
    widths = (A_HEADS * LANES, IDX_HEADS * LANES, 4 * LANES, b_width, b_width, b_width, c_width)
    return pl.pallas_call(
        functools.partial(_in_proj_kernel, a_q_rank=a_q_rank, b_width=b_width),
        grid=(n // tm,),
        in_specs=[row(d), _resident(g.shape), _resident(w.shape), _resident(gcq.shape),
                  _resident(wq.shape), tab, tab],
        out_specs=[row(wd) for wd in widths],
        out_shape=[jax.ShapeDtypeStruct((n, wd), BF16) for wd in widths],
        compiler_params=_params(("parallel",)),
        name="in_proj",
    )(x2, g, w, gcq, wq, cos, sin_signed)


def _attend(q_h, k_block, bias_block, vt_block, l_scr, nblocks):
    def logits_pass(kb, m):
        lg = lax.dot_general(k_block(kb), q_h, _NT, preferred_element_type=F32) + bias_block(kb)
        l_scr[kb] = lg
        return jnp.maximum(m, jnp.max(lg, axis=0, keepdims=True))

    m = lax.fori_loop(0, nblocks, logits_pass, jnp.full((1, TQ), NEG, F32))

    def value_pass(kb, acc):
        p = jnp.exp(l_scr[kb] - m).astype(BF16)
        return acc + jnp.dot(vt_block(kb), p, preferred_element_type=F32)

    acc = lax.fori_loop(0, nblocks, value_pass, jnp.zeros((LANES, TQ), F32))
    acc_t = acc.T
    return acc_t[:, :HEAD_DIM] / acc_t[:, HEAD_DIM:HEAD_DIM + 1]


def _ones_rows_from(vt, first_row):
    row = lax.broadcasted_iota(jnp.int32, vt.shape, 0)
    return jnp.where(row >= first_row, 1.0, vt)


def _dsa_kernel(qi_ref, qa_ref, kvw_ref, o_ref, key_scr, bias_scr, l_scr, vt_scr, *, topk):
    n = pl.program_id(1)
    nkb = key_scr.shape[0]

    @pl.when(n == 0)
    def _():
        for kb in range(nkb):
            vt = kvw_ref[kb * TK:(kb + 1) * TK, 2 * LANES:3 * LANES].astype(F32).T
            vt_scr[kb] = _ones_rows_from(vt, HEAD_DIM).astype(BF16)

    nblocks = n + 1
    q0 = pl.multiple_of(n * TQ, TQ)
    w_t = kvw_ref[pl.ds(q0, TQ), 3 * LANES:4 * LANES].astype(F32).T
    s_loc = lax.broadcasted_iota(jnp.int32, (TK, TQ), 0)
    t_loc = lax.broadcasted_iota(jnp.int32, (TK, TQ), 1)

    def rows(kb, c0):
        return kvw_ref[pl.ds(pl.multiple_of(kb * TK, TK), TK), c0:c0 + LANES]

    def score_block(kb, carry):
        k_idx = rows(kb, LANES)
        acc = jnp.zeros((TK, TQ), F32)
        for h in range(IDX_HEADS):
            d = lax.dot_general(k_idx, qi_ref[:, h * LANES:(h + 1) * LANES], _NT,
                                preferred_element_type=F32)
            acc = acc + jnp.maximum(d, 0.0) * w_t[h:h + 1, :]
        bits = pltpu.bitcast(acc, jnp.int32)
        key = bits ^ ((bits >> 31) & jnp.int32(0x7FFFFFFF))
        admissible = (kb - n) * TK + s_loc <= t_loc
        key_scr[kb] = jnp.where(admissible, key, jnp.int32(INT_MIN))
        return carry

    lax.fori_loop(0, nblocks, score_block, 0)

    def count_ge(cand):
        def body(kb, c):
            return c + jnp.sum(jnp.where(key_scr[kb] >= cand, 1.0, 0.0), axis=0, keepdims=True)
        return lax.fori_loop(0, nblocks, body, jnp.zeros((1, TQ), F32))

    zero = jnp.zeros((1, TQ), jnp.int32)
    thr = jnp.where(count_ge(zero) >= topk, zero, jnp.int32(INT_MIN))

    def bit_step(i, thr):
        cand = thr | jnp.left_shift(jnp.int32(1), 30 - i)
        return jnp.where(count_ge(cand) >= topk, cand, thr)

    thr = lax.fori_loop(0, 31, bit_step, thr)
    thr = jnp.maximum(thr, jnp.int32(INT_MIN + 1))

    def bias_block(kb, carry):
        bias_scr[kb] = jnp.where(key_scr[kb] >= thr, 0.0, NEG)
        return carry

    lax.fori_loop(0, nblocks, bias_block, 0)

    for h in range(A_HEADS):
        o = _attend(qa_ref[:, h * LANES:(h + 1) * LANES], lambda kb: rows(kb, 0),
                    lambda kb: bias_scr[kb], lambda kb: vt_scr[kb], l_scr, nblocks)
        o_ref[:, h * HEAD_DIM:(h + 1) * HEAD_DIM] = o.astype(o_ref.dtype)


def _dsa(q_idx, q_a, kvw, *, batch, seq):
    nq, nkb = seq // TQ, seq // TK
    topk = min(TOPK_MAX, seq // 4)
    blk = lambda width: pl.BlockSpec((TQ, width), lambda b, n: (b * nq + n, 0))
    return pl.pallas_call(
        functools.partial(_dsa_kernel, topk=topk),
        grid=(batch, nq),
        in_specs=[blk(IDX_HEADS * LANES), blk(A_HEADS * LANES),
                  pl.BlockSpec((seq, 4 * LANES), lambda b, n: (b, 0))],
        out_specs=blk(A_HEADS * HEAD_DIM),
        out_shape=jax.ShapeDtypeStruct((batch * seq, A_HEADS * HEAD_DIM), BF16),
        scratch_shapes=[pltpu.VMEM((nkb, TK, TQ), jnp.int32), pltpu.VMEM((nkb, TK, TQ), F32),
                        pltpu.VMEM((nkb, TK, TQ), F32), pltpu.VMEM((nkb, LANES, TK), BF16)],
        compiler_params=_params(("parallel", "arbitrary")),
        name="dsa_attention",
    )(q_idx, q_a, kvw)


def _dilated_kernel(q_ref, k_ref, v_ref, bias_ref, o_ref, l_scr, vt_scr):
    n = pl.program_id(2)
    nkb = l_scr.shape[0]

    @pl.when(n == 0)
    def _():
        for kb in range(nkb):
            vt = v_ref[kb * TK:(kb + 1) * TK, :].astype(F32).T
            vt_scr[0, kb] = _ones_rows_from(vt, HEAD_DIM).astype(BF16)
            vt_scr[1, kb] = jnp.concatenate(
                [vt[HEAD_DIM:], jnp.ones((LANES - HEAD_DIM, TK), F32)], axis=0).astype(BF16)

    nblocks = n + 1
    q = q_ref[...].astype(F32)
    lane = lax.broadcasted_iota(jnp.int32, q.shape, 1)
    for hh in range(2):
        q_h = jnp.where((lane // HALF) % 2 == hh, q, 0.0).astype(BF16)
        o = _attend(q_h, lambda kb: k_ref[pl.ds(pl.multiple_of(kb * TK, TK), TK), :],
                    lambda kb: bias_ref[n - kb], lambda kb: vt_scr[hh, kb], l_scr, nblocks)
        o_ref[:, hh * HEAD_DIM:(hh + 1) * HEAD_DIM] = o.astype(o_ref.dtype)


def _dilated(q_b, k_b, v_b, bias, *, batch, seq):
    nq, nkb = seq // TQ, seq // TK
    qblk = pl.BlockSpec((TQ, LANES), lambda b, p, n: (b * nq + n, p))
    kvblk = pl.BlockSpec((seq, LANES), lambda b, p, n: (b, p))
    return pl.pallas_call(
        _dilated_kernel,
        grid=(batch, B_PAIRS, nq),
        in_specs=[qblk, kvblk, kvblk, _resident(bias.shape)],
        out_specs=qblk,
        out_shape=jax.ShapeDtypeStruct(q_b.shape, BF16),
        scratch_shapes=[pltpu.VMEM((nkb, TK, TQ), F32), pltpu.VMEM((2, nkb, LANES, TK), BF16)],
        compiler_params=_params(("parallel", "parallel", "arbitrary")),
        name="dilated_attention",
    )(q_b, k_b, v_b, bias)


def _out_proj_kernel(x_ref, oa_ref, ob_ref, uc_ref, uprev_ref, wcol_ref, wpool_ref, pscale_ref,
                     wo_ref, o_ref, *, seq):
    tm = x_ref.shape[0]
    pos0 = (pl.program_id(0) * tm) % seq
    u = uc_ref[...].astype(F32)
    hist = jnp.where(pos0 == 0, 0.0, uprev_ref[...].astype(F32))
    ext = jnp.concatenate([hist, u], axis=0)
    s2 = ext + pltpu.roll(ext, 1, axis=0)
    s4 = s2 + pltpu.roll(s2, 2, axis=0)
    s8 = s4 + pltpu.roll(s4, 4, axis=0)
    s16 = s8 + pltpu.roll(s8, 8, axis=0)
    wcol = wcol_ref[...]
    sums = jnp.where(wcol == 2.0, s2, jnp.where(wcol == 4.0, s4, jnp.where(wcol == 8.0, s8, s16)))
    pos = (pos0 + lax.broadcasted_iota(jnp.int32, (tm, 1), 0)).astype(F32)
    y = sums[POOL_HIST:] / jnp.minimum(pos + 1.0, wcol) - u
    oc = (jnp.dot(y.astype(BF16), wpool_ref[...], preferred_element_type=F32) * pscale_ref[...]).astype(BF16)
    a_w, b_w = oa_ref.shape[1], ob_ref.shape[1]
    o_ref[...] = (x_ref[...]
                  + jnp.dot(oa_ref[...], wo_ref[:a_w], preferred_element_type=F32)
                  + jnp.dot(ob_ref[...], wo_ref[a_w:a_w + b_w], preferred_element_type=F32)
                  + jnp.dot(oc, wo_ref[a_w + b_w:], preferred_element_type=F32))


def _out_proj(x2, o_a, o_b, u_c, wcol, wpool, pscale, wo, *, seq, tm):
    n, d = x2.shape
    row = lambda a: pl.BlockSpec((tm, a.shape[1]), lambda i: (i, 0))
    hist = pl.BlockSpec((POOL_HIST, u_c.shape[1]),
                        lambda i: (jnp.maximum(i * (tm // POOL_HIST) - 1, 0), 0))
    return pl.pallas_call(
        functools.partial(_out_proj_kernel, seq=seq),
        grid=(n // tm,),
        in_specs=[row(x2), row(o_a), row(o_b), row(u_c), hist, _resident(wcol.shape),
                  _resident(wpool.shape), _resident(pscale.shape), _resident(wo.shape)],
        out_specs=row(x2),
        out_shape=jax.ShapeDtypeStruct((n, d), F32),
        compiler_params=_params(("parallel",)),
        name="out_proj",
    )(x2, o_a, o_b, u_c, u_c, wcol, wpool, pscale, wo)


def _mlp_kernel(x_ref, g_ref, wup_ref, wdn_ref, gfin_ref, o_ref, h_scr, acc_scr, *, final):
    j = pl.program_id(1)

    @pl.when(j == 0)
    def _():
        x = x_ref[...]
        h_scr[...] = (x * lax.rsqrt(jnp.mean(x * x, axis=-1, keepdims=True) + EPS) * g_ref[...]).astype(BF16)
        acc_scr[...] = x

    a = jnp.dot(h_scr[...], wup_ref[...], preferred_element_type=F32)
    a = jnp.square(jnp.maximum(a, 0.0)).astype(BF16)
    acc_scr[...] += jnp.dot(a, wdn_ref[...], preferred_element_type=F32)

    @pl.when(j == pl.num_programs(1) - 1)
    def _():
        y = acc_scr[...]
        if final:
            y = y * lax.rsqrt(jnp.mean(y * y, axis=-1, keepdims=True) + EPS) * gfin_ref[...]
        o_ref[...] = y


def _mlp(x2, g, wup, wdn, gfin, *, tm, tf, final):
    n, d = x2.shape
    dff = wup.shape[1]
    return pl.pallas_call(
        functools.partial(_mlp_kernel, final=final),
        grid=(n // tm, dff // tf),
        in_specs=[pl.BlockSpec((tm, d), lambda i, j: (i, 0)), _resident(g.shape),
                  pl.BlockSpec((d, tf), lambda i, j: (0, j)), pl.BlockSpec((tf, d), lambda i, j: (j, 0)),
                  _resident(gfin.shape)],
        out_specs=pl.BlockSpec((tm, d), lambda i, j: (i, 0)),
        out_shape=jax.ShapeDtypeStruct((n, d), F32),
        scratch_shapes=[pltpu.VMEM((tm, d), BF16), pltpu.VMEM((tm, d), F32)],
        compiler_params=_params(("parallel", "arbitrary")),
        name="mlp",
    )(x2, g, wup, wdn, gfin)


def kernel(x, g_mix, w_in, g_cq, w_uq, w_uq_idx, w_pool, pool_scale, w_o, g_mlp, w_up, w_down, g_final):
    batch, seq, d_model = x.shape
    depth = w_in.shape[0]
    assert seq % TQ == 0 and TQ == TK
    cols, col_scale, a_q_rank, b_width, c_width = _in_layout(d_model)
    cos, sin_signed = _rope_tables(seq)
    dil_bias = _dilated_bias(seq // TK)
    groups, gdim = w_pool.shape[1], w_pool.shape[2]
    wcol = jnp.asarray(np.repeat(np.asarray(POOL_WINDOWS, np.float32), gdim)[None, :])
    qa_cols = np.concatenate([_padded_head_cols(h * HEAD_DIM) for h in range(A_HEADS)])
    qi_cols = np.concatenate([_padded_head_cols(h * HEAD_DIM) for h in range(IDX_HEADS)])

    x2 = x.reshape(batch * seq, d_model)
    for l in range(depth):
        w_in_l = (_take_cols(w_in[l], cols) * col_scale[None, :]).astype(BF16)
        wq_l = jnp.concatenate([_take_cols(w_uq[l], qa_cols) * (HEAD_DIM ** -0.5),
                                _take_cols(w_uq_idx[l], qi_cols)], axis=1).astype(BF16)
        q_a, q_idx, kvw, q_b, k_b, v_b, u_c = _in_proj(
            x2, g_mix[l][None, :], w_in_l, g_cq[l][None, :], wq_l, cos, sin_signed,
            seq=seq, tm=256, a_q_rank=a_q_rank, b_width=b_width, c_width=c_width)
        o_a = _dsa(q_idx, q_a, kvw, batch=batch, seq=seq)
        o_b = _dilated(q_b, k_b, v_b, dil_bias, batch=batch, seq=seq)
        wpool_l = jax.scipy.linalg.block_diag(*[w_pool[l, g] for g in range(groups)]).astype(BF16)
        x2 = _out_proj(x2, o_a, o_b, u_c, wcol, wpool_l, pool_scale[l][None, :], w_o[l].astype(BF16),
                       seq=seq, tm=256)
        x2 = _mlp(x2, g_mlp[l][None, :], w_up[l].astype(BF16), w_down[l].astype(BF16),
                  g_final[None, :], tm=512, tf=512, final=(l == depth - 1))
    return x2.reshape(batch, seq, d_model)
```

```python
import functools

import numpy as np
import jax
import jax.numpy as jnp
from jax import lax
from jax.experimental import pallas as pl
from jax.experimental.pallas import tpu as pltpu

F32 = jnp.float32
BF16 = jnp.bfloat16

LANES = 128
VMEM_LIMIT = 56 * 1024 * 1024

HEAD_DIM = 64
HALF = HEAD_DIM // 2
ROPE_THETA = 10000.0
EPS = 1e-6
NEG = -1e30
INT_MIN = -(2 ** 31)
A_HEADS = 8
IDX_HEADS = 16
TOPK_MAX = 256
B_HEADS = 12
B_PAIRS = B_HEADS // 2
DILATED_PATTERNS = ((128, 1), (512, 4), (2048, 16))
POOL_WINDOWS = (2, 4, 8, 16)
POOL_HIST = 16

TQ = 256
TK = 256
_NT = (((1,), (1,)), ((), ()))


def _params(sem):
    return pltpu.CompilerParams(dimension_semantics=sem, vmem_limit_bytes=VMEM_LIMIT)


def _resident(shape):
    return pl.BlockSpec(shape, lambda *_: (0,) * len(shape), pipeline_mode=pl.Buffered(1))


def _padded_head_cols(base):
    cols = np.full((LANES,), -1, np.int64)
    cols[0:HALF] = base + np.arange(HALF)
    cols[2 * HALF:3 * HALF] = base + HALF + np.arange(HALF)
    return cols


def _pair_head_cols(base):
    h0, h1 = base, base + HEAD_DIM
    return np.concatenate([h0 + np.arange(HALF), h1 + np.arange(HALF),
                           h0 + HALF + np.arange(HALF), h1 + HALF + np.arange(HALF)])


def _take_cols(w, cols):
    pieces, start = [], 0
    for end in range(1, len(cols) + 1):
        run_ends = end == len(cols) or (cols[end] != cols[end - 1] + 1 if cols[end - 1] >= 0
                                        else cols[end] >= 0)
        if run_ends:
            c0 = int(cols[start])
            pieces.append(jnp.zeros((w.shape[0], end - start), w.dtype) if c0 < 0
                          else w[:, c0:c0 + end - start])
            start = end
    return jnp.concatenate(pieces, axis=1)


def _in_layout(d_model):
    a_q_rank = 3 * d_model // 16
    b_width = B_HEADS * HEAD_DIM
    c_width = 3 * d_model // 8
    sizes = (a_q_rank, HEAD_DIM, HEAD_DIM, HEAD_DIM, IDX_HEADS, b_width, b_width, b_width, c_width)
    off = np.concatenate([[0], np.cumsum(sizes)])
    o_cq, o_ka, o_va, o_ki, o_wi, o_qb, o_kb, o_vb, o_uc = (int(v) for v in off[:-1])
    pad = np.full((LANES,), -1, np.int64)
    va = pad.copy()
    va[:HEAD_DIM] = o_va + np.arange(HEAD_DIM)
    wi = pad.copy()
    wi[:IDX_HEADS] = o_wi + np.arange(IDX_HEADS)
    cols = np.concatenate(
        [o_cq + np.arange(a_q_rank),
         _padded_head_cols(o_ka), _padded_head_cols(o_ki), va, wi]
        + [_pair_head_cols(o_qb + p * LANES) for p in range(B_PAIRS)]
        + [_pair_head_cols(o_kb + p * LANES) for p in range(B_PAIRS)]
        + [o_vb + np.arange(b_width), o_uc + np.arange(c_width)])
    scale = np.ones((cols.shape[0],), np.float32)
    wi_start = a_q_rank + 3 * LANES
    scale[wi_start:wi_start + LANES] = (IDX_HEADS ** -0.5) * (HEAD_DIM ** -0.5)
    qb_start = a_q_rank + 4 * LANES
    scale[qb_start:qb_start + b_width] = HEAD_DIM ** -0.5
    return cols, scale, a_q_rank, b_width, c_width


def _rope_tables(seq):
    inv = ROPE_THETA ** (-jnp.arange(HALF, dtype=F32) / HALF)
    ang = jnp.arange(seq, dtype=F32)[:, None] * inv[None, :]
    cos, sin = jnp.cos(ang), jnp.sin(ang)
    return (jnp.concatenate([cos, cos, cos, cos], axis=1),
            jnp.concatenate([-sin, -sin, sin, sin], axis=1))


def _dilated_bias(nkb):
    d = np.arange(nkb)[:, None, None]
    delta = d * TK + np.arange(TQ)[None, None, :] - np.arange(TK)[None, :, None]
    count = np.zeros(delta.shape, np.float64)
    for window, dil in DILATED_PATTERNS:
        count += (delta >= 0) & (delta % dil == 0) & (delta <= window)
    with np.errstate(divide="ignore"):
        bias = np.where(count > 0, np.log(np.maximum(count, 1.0)), NEG)
    return jnp.asarray(bias, F32)


def _rope_store(dst_ref, x, cos, sin_signed):
    for j in range(x.shape[1] // LANES):
        xb = x[:, j * LANES:(j + 1) * LANES]
        y = (xb * cos + pltpu.roll(xb, 2 * HALF, axis=1) * sin_signed).astype(dst_ref.dtype)
        if len(dst_ref.shape) == 3:
            dst_ref[j] = y
        else:
            dst_ref[:, j * LANES:(j + 1) * LANES] = y


def _in_proj_kernel(x_ref, g_ref, w_ref, gcq_ref, wq_ref, cos_ref, sin_ref,
                    qa_ref, qi_ref, kvw_ref, qb_ref, kb_ref, vb_ref, uc_ref,
                    *, a_q_rank, b_width):
    x = x_ref[...]
    h = (x * lax.rsqrt(jnp.mean(x * x, axis=-1, keepdims=True) + EPS) * g_ref[...]).astype(BF16)
    cos, sin_signed = cos_ref[...], sin_ref[...]

    def proj(start, width):
        return jnp.dot(h, w_ref[:, start:start + width], preferred_element_type=F32)

    cq = proj(0, a_q_rank)
    cqn = (cq * lax.rsqrt(jnp.mean(cq * cq, axis=-1, keepdims=True) + EPS) * gcq_ref[...]).astype(BF16)
    qa_w = A_HEADS * LANES
    _rope_store(qa_ref, jnp.dot(cqn, wq_ref[:, :qa_w], preferred_element_type=F32), cos, sin_signed)
    _rope_store(qi_ref, jnp.dot(cqn, wq_ref[:, qa_w:], preferred_element_type=F32), cos, sin_signed)

    kvw = proj(a_q_rank, 4 * LANES)
    _rope_store(kvw_ref.at[:, :2 * LANES], kvw[:, :2 * LANES], cos, sin_signed)
    kvw_ref[:, 2 * LANES:] = kvw[:, 2 * LANES:].astype(BF16)
    start = a_q_rank + 4 * LANES
    _rope_store(qb_ref, proj(start, b_width), cos, sin_signed)
    _rope_store(kb_ref, proj(start + b_width, b_width), cos, sin_signed)
    vb_ref[...] = proj(start + 2 * b_width, b_width).astype(BF16)
    uc_ref[...] = proj(start + 3 * b_width, uc_ref.shape[1]).astype(BF16)


def _in_proj(x2, g, w, gcq, wq, cos, sin_signed, *, batch, seq, a_q_rank, b_width, c_width):
    n, d = x2.shape
    tm, nq = TQ, seq // TQ
    row = lambda width: pl.BlockSpec((tm, width), lambda i: (i, 0))
    tab = pl.BlockSpec((tm, LANES), lambda i: (i % nq, 0))
    heads = lambda nh: pl.BlockSpec((None, nh, tm, LANES), lambda i: (i // nq, 0, i % nq, 0))
    qshape = lambda nh: jax.ShapeDtypeStruct((batch, nh, seq, LANES), BF16)
    widths = (4 * LANES, b_width, b_width, b_width, c_width)
    return pl.pallas_call(
        functools.partial(_in_proj_kernel, a_q_rank=a_q_rank, b_width=b_width),
        grid=(n // tm,),
        in_specs=[row(d), _resident(g.shape), _resident(w.shape), _resident(gcq.shape),
                  _resident(wq.shape), tab, tab],
        out_specs=[heads(A_HEADS), heads(IDX_HEADS)] + [row(wd) for wd in widths],
        out_shape=[qshape(A_HEADS), qshape(IDX_HEADS)]
                  + [jax.ShapeDtypeStruct((n, wd), BF16) for wd in widths],
        compiler_params=_params(("parallel",)),
        name="in_proj",
    )(x2, g, w, gcq, wq, cos, sin_signed)


def _attend_heads(nheads, q_of, k_of, bias_block, vt_of, raw_scr, acc_scr, ot_scr, last):
    def qk(kb, slot):
        for h in range(nheads):
            raw_scr[slot, h] = lax.dot_general(k_of(h, kb), q_of(h), _NT, preferred_element_type=F32)

    def softmax_pv(kb, slot, ms):
        bias = bias_block(kb)
        new = []
        for h in range(nheads):
            lg = raw_scr[slot, h] + bias
            m_new = jnp.maximum(ms[h], jnp.max(lg, axis=0, keepdims=True))
            p = jnp.exp((lg - m_new).astype(BF16))
            acc_scr[h] = (acc_scr[h] * jnp.exp(ms[h] - m_new)
                          + jnp.dot(vt_of(h, kb), p, preferred_element_type=F32))
            new.append(m_new)
        return tuple(new)

    def two_blocks(i, ms):
        kb = 2 * i
        qk(kb + 1, 1)
        ms = softmax_pv(kb, 0, ms)
        qk(kb + 2, 0)
        return softmax_pv(kb + 1, 1, ms)

    acc_scr[...] = jnp.zeros(acc_scr.shape, F32)
    qk(0, 0)
    pairs = last // 2
    ms = lax.fori_loop(0, pairs, two_blocks, (jnp.full((1, TQ), NEG, F32),) * nheads)
    ms = softmax_pv(2 * pairs, 0, ms)

    @pl.when(last % 2 == 1)
    def _():
        qk(last, 1)
        softmax_pv(last, 1, ms)

    for h in range(nheads):
        acc = acc_scr[h]
        ot_scr[h * HEAD_DIM:(h + 1) * HEAD_DIM, :] = acc[:HEAD_DIM] * (1.0 / acc[HEAD_DIM:HEAD_DIM + 1])


def _ones_rows_from(vt, first_row):
    row = lax.broadcasted_iota(jnp.int32, vt.shape, 0)
    return jnp.where(row >= first_row, 1.0, vt)


def _store_transposed(o_ref, ot_scr):
    for c in range(ot_scr.shape[0] // LANES):
        o_ref[:, c * LANES:(c + 1) * LANES] = ot_scr[c * LANES:(c + 1) * LANES, :].T.astype(o_ref.dtype)


def _select_threshold(key_scr, hi_scr, lo_scr, ntiles, topk):
    i16_min, i16_max = -(2 ** 15), 2 ** 15 - 1
    slab = 16

    def count_ge(scr, cand):
        c16 = cand.astype(jnp.int16)
        part = jnp.zeros((slab, TQ), jnp.int16)
        for i in range(ntiles):
            m = jnp.where(scr[i] >= c16, jnp.int16(1), jnp.int16(0))
            for r in range(TK // slab):
                part = part + m[r * slab:(r + 1) * slab]
        return jnp.sum(part.astype(jnp.int32), axis=0, keepdims=True)

    def radix16(scr, need, n_all):
        def refine(state, cand):
            thr, n_ge = state
            c = count_ge(scr, cand)
            ok = c >= need
            return jnp.where(ok, cand, thr), jnp.where(ok, c, n_ge)

        state = refine((jnp.full((1, TQ), i16_min, jnp.int32), n_all), jnp.zeros((1, TQ), jnp.int32))
        return lax.fori_loop(
            0, 15, lambda b, st: refine(st, st[0] | jnp.left_shift(jnp.int32(1), 14 - b)), state)

    for i in range(ntiles):
        hi_scr[i] = (key_scr[i] >> 16).astype(jnp.int16)
    n_all = jnp.full((1, TQ), ntiles * TK, jnp.int32)
    t_hi, n_hi = radix16(hi_scr, topk, n_all)
    n_above = jnp.where(t_hi == i16_max, 0, count_ge(hi_scr, jnp.minimum(t_hi + 1, i16_max)))

    t_hi16 = t_hi.astype(jnp.int16)
    for i in range(ntiles):
        lo = ((key_scr[i] & 0xFFFF) + i16_min).astype(jnp.int16)
        lo_scr[i] = jnp.where(hi_scr[i] == t_hi16, lo, jnp.int16(i16_min))
    t_lo, n_lo = radix16(lo_scr, topk - n_above, n_hi - n_above)
    return jnp.left_shift(t_hi, 16) | (t_lo - i16_min), n_above + n_lo


def _dsa_kernel(qi_ref, qa_ref, kvw_ref, o_ref, key_scr, hi_scr, lo_scr, bias_scr, raw_scr, vt_scr,
                acc_scr, ot_scr, thr_scr, cnt_scr, *, topk):
    n = pl.program_id(1)
    nq = key_scr.shape[0]

    @pl.when(n == 0)
    def _():
        for kb in range(nq):
            vt = kvw_ref[kb * TK:(kb + 1) * TK, 2 * LANES:3 * LANES].astype(F32).T
            vt_scr[kb] = _ones_rows_from(vt, HEAD_DIM).astype(BF16)

    def rows(kb, c0):
        return kvw_ref[pl.ds(pl.multiple_of(kb * TK, TK), TK), c0:c0 + LANES]

    w_t = kvw_ref[pl.ds(pl.multiple_of(n * TQ, TQ), TQ), 3 * LANES:4 * LANES].astype(F32).T
    s_loc = lax.broadcasted_iota(jnp.int32, (TK, TQ), 0)
    t_loc = lax.broadcasted_iota(jnp.int32, (TK, TQ), 1)

    def score_block(kb, carry):
        k_idx = rows(kb, LANES)
        acc = jnp.zeros((TK, TQ), F32)
        for h in range(IDX_HEADS):
            d = lax.dot_general(k_idx, qi_ref[h], _NT, preferred_element_type=F32)
            acc = acc + jnp.maximum(d, 0.0) * w_t[h:h + 1, :]
        bits = pltpu.bitcast(acc, jnp.int32)
        key = bits ^ ((bits >> 31) & jnp.int32(0x7FFFFFFF))
        admissible = (kb - n) * TK + s_loc <= t_loc
        key_scr[kb] = jnp.where(admissible, key, jnp.int32(INT_MIN))
        return carry

    lax.fori_loop(0, n + 1, score_block, 0)

    for c in range(nq):
        @pl.when(n == c)
        def _(c=c):
            thr, n_ge = _select_threshold(key_scr, hi_scr, lo_scr, c + 1, topk)
            thr_scr[...] = jnp.broadcast_to(thr, thr_scr.shape)
            cnt_scr[...] = jnp.broadcast_to(n_ge, cnt_scr.shape)

    thr = thr_scr[0:1, :]
    n_ge = cnt_scr[0:1, :]
    floor = jnp.maximum(thr, jnp.int32(INT_MIN + 1))

    def bias_block(kb, carry):
        bias_scr[kb] = jnp.where(key_scr[kb] >= floor, 0.0, NEG)
        return carry

    lax.fori_loop(0, n + 1, bias_block, 0)

    tied = jnp.where(thr != INT_MIN, jnp.where(n_ge > topk, 1.0, 0.0), 0.0)

    @pl.when(jnp.max(tied) > 0.0)
    def _():
        before = (lax.broadcasted_iota(jnp.int32, (TK, TK), 0)
                  > lax.broadcasted_iota(jnp.int32, (TK, TK), 1)).astype(BF16)

        def count_eq(kb, c):
            return c + jnp.sum(jnp.where(key_scr[kb] == thr, 1.0, 0.0), axis=0, keepdims=True)

        n_eq = lax.fori_loop(0, n + 1, count_eq, jnp.zeros((1, TQ), F32))
        need = topk - (n_ge.astype(F32) - n_eq)

        def rebias(kb, seen):
            key = key_scr[kb]
            eq = jnp.where(key == thr, 1.0, 0.0)
            rank = jnp.dot(before, eq.astype(BF16), preferred_element_type=F32) + seen
            keep = jnp.where(key > thr, 1.0, jnp.where(rank < need, eq, 0.0))
            bias_scr[kb] = jnp.where(tied > 0.0, jnp.where(keep > 0.0, 0.0, NEG), bias_scr[kb])
            return seen + jnp.sum(eq, axis=0, keepdims=True)

        lax.fori_loop(0, n + 1, rebias, jnp.zeros((1, TQ), F32))

    _attend_heads(A_HEADS, lambda h: qa_ref[h], lambda h, kb: rows(kb, 0), lambda kb: bias_scr[kb],
                  lambda h, kb: vt_scr[kb], raw_scr, acc_scr, ot_scr, n)
    _store_transposed(o_ref, ot_scr)


def _dsa(q_idx, q_a, kvw, *, batch, seq):
    nq = seq // TQ
    topk = min(TOPK_MAX, seq // 4)
    qblk = lambda nh: pl.BlockSpec((None, nh, TQ, LANES), lambda b, n: (b, 0, n, 0))
    width = A_HEADS * HEAD_DIM
    tile = lambda dtype: pltpu.VMEM((nq, TK, TQ), dtype)
    return pl.pallas_call(
        functools.partial(_dsa_kernel, topk=topk),
        grid=(batch, nq),
        in_specs=[qblk(IDX_HEADS), qblk(A_HEADS), pl.BlockSpec((seq, 4 * LANES), lambda b, n: (b, 0))],
        out_specs=pl.BlockSpec((TQ, width), lambda b, n: (b * nq + n, 0)),
        out_shape=jax.ShapeDtypeStruct((batch * seq, width), BF16),
        scratch_shapes=[tile(jnp.int32), tile(jnp.int16), tile(jnp.int16), tile(F32),
                        pltpu.VMEM((2, A_HEADS, TK, TQ), F32), pltpu.VMEM((nq, LANES, TK), BF16),
                        pltpu.VMEM((A_HEADS, LANES, TQ), F32), pltpu.VMEM((width, TQ), F32),
                        pltpu.VMEM((8, TQ), jnp.int32), pltpu.VMEM((8, TQ), jnp.int32)],
        compiler_params=_params(("parallel", "arbitrary")),
        name="dsa_attention",
    )(q_idx, q_a, kvw)


DIL_PAIRS = 6


def _dilated_kernel(q_ref, k_ref, v_ref, bias_ref, o_ref, qm_scr, raw_scr, vt_scr, acc_scr, ot_scr):
    n = pl.program_id(2)
    nq = vt_scr.shape[1]
    nheads = 2 * DIL_PAIRS

    @pl.when(n == 0)
    def _():
        for p in range(DIL_PAIRS):
            for kb in range(nq):
                vt = v_ref[kb * TK:(kb + 1) * TK, p * LANES:(p + 1) * LANES].astype(F32).T
                vt_scr[2 * p, kb] = _ones_rows_from(vt, HEAD_DIM).astype(BF16)
                vt_scr[2 * p + 1, kb] = jnp.concatenate(
                    [vt[HEAD_DIM:], jnp.ones((LANES - HEAD_DIM, TK), F32)], axis=0).astype(BF16)

    lane = lax.broadcasted_iota(jnp.int32, (TQ, LANES), 1)
    for h in range(nheads):
        q = q_ref[:, (h // 2) * LANES:(h // 2 + 1) * LANES].astype(F32)
        qm_scr[h] = jnp.where((lane // HALF) % 2 == h % 2, q, 0.0).astype(BF16)

    def k_of(h, kb):
        return k_ref[pl.ds(pl.multiple_of(kb * TK, TK), TK), (h // 2) * LANES:(h // 2 + 1) * LANES]

    _attend_heads(nheads, lambda h: qm_scr[h], k_of, lambda kb: bias_ref[n - kb],
                  lambda h, kb: vt_scr[h, kb], raw_scr, acc_scr, ot_scr, n)
    _store_transposed(o_ref, ot_scr)


def _dilated(q_b, k_b, v_b, bias, *, batch, seq):
    nq = seq // TQ
    width = DIL_PAIRS * LANES
    nheads = 2 * DIL_PAIRS
    qblk = pl.BlockSpec((TQ, width), lambda b, g, n: (b * nq + n, g))
    kvblk = pl.BlockSpec((seq, width), lambda b, g, n: (b, g))
    return pl.pallas_call(
        _dilated_kernel,
        grid=(batch, B_PAIRS // DIL_PAIRS, nq),
        in_specs=[qblk, kvblk, kvblk, _resident(bias.shape)],
        out_specs=qblk,
        out_shape=jax.ShapeDtypeStruct(q_b.shape, BF16),
        scratch_shapes=[pltpu.VMEM((nheads, TQ, LANES), BF16), pltpu.VMEM((2, nheads, TK, TQ), F32),
                        pltpu.VMEM((nheads, nq, LANES, TK), BF16), pltpu.VMEM((nheads, LANES, TQ), F32),
                        pltpu.VMEM((nheads * HEAD_DIM, TQ), F32)],
        compiler_params=_params(("parallel", "parallel", "arbitrary")),
        name="dilated_attention",
    )(q_b, k_b, v_b, bias)


def _out_proj_kernel(x_ref, oa_ref, ob_ref, uc_ref, uprev_ref, wcol_ref, wpool_ref, pscale_ref,
                     wo_ref, o_ref, *, seq):
    tm = x_ref.shape[0]
    pos0 = (pl.program_id(0) * tm) % seq
    u = uc_ref[...].astype(F32)
    hist = jnp.where(pos0 == 0, 0.0, uprev_ref[...].astype(F32))
    ext = jnp.concatenate([hist, u], axis=0)
    s2 = ext + pltpu.roll(ext, 1, axis=0)
    s4 = s2 + pltpu.roll(s2, 2, axis=0)
    s8 = s4 + pltpu.roll(s4, 4, axis=0)
    s16 = s8 + pltpu.roll(s8, 8, axis=0)
    wcol = wcol_ref[...]
    sums = jnp.where(wcol == 2.0, s2, jnp.where(wcol == 4.0, s4, jnp.where(wcol == 8.0, s8, s16)))
    pos = (pos0 + lax.broadcasted_iota(jnp.int32, (tm, 1), 0)).astype(F32)
    y = sums[POOL_HIST:] / jnp.minimum(pos + 1.0, wcol) - u
    oc = (jnp.dot(y.astype(BF16), wpool_ref[...], preferred_element_type=F32) * pscale_ref[...]).astype(BF16)
    a_w, b_w = oa_ref.shape[1], ob_ref.shape[1]
    o_ref[...] = (x_ref[...]
                  + jnp.dot(oa_ref[...], wo_ref[:a_w], preferred_element_type=F32)
                  + jnp.dot(ob_ref[...], wo_ref[a_w:a_w + b_w], preferred_element_type=F32)
                  + jnp.dot(oc, wo_ref[a_w + b_w:], preferred_element_type=F32))


def _out_proj(x2, o_a, o_b, u_c, wcol, wpool, pscale, wo, *, seq):
    n, d = x2.shape
    tm = TQ
    row = lambda a: pl.BlockSpec((tm, a.shape[1]), lambda i: (i, 0))
    hist = pl.BlockSpec((POOL_HIST, u_c.shape[1]),
                        lambda i: (jnp.maximum(i * (tm // POOL_HIST) - 1, 0), 0))
    return pl.pallas_call(
        functools.partial(_out_proj_kernel, seq=seq),
        grid=(n // tm,),
        in_specs=[row(x2), row(o_a), row(o_b), row(u_c), hist, _resident(wcol.shape),
                  _resident(wpool.shape), _resident(pscale.shape), _resident(wo.shape)],
        out_specs=row(x2),
        out_shape=jax.ShapeDtypeStruct((n, d), F32),
        compiler_params=_params(("parallel",)),
        name="out_proj",
    )(x2, o_a, o_b, u_c, u_c, wcol, wpool, pscale, wo)


def _mlp_kernel(x_ref, g_ref, wup_ref, wdn_ref, gfin_ref, o_ref, h_scr, acc_scr, *, final):
    j = pl.program_id(1)

    @pl.when(j == 0)
    def _():
        x = x_ref[...]
        h_scr[...] = (x * lax.rsqrt(jnp.mean(x * x, axis=-1, keepdims=True) + EPS) * g_ref[...]).astype(BF16)
        acc_scr[...] = x

    a = jnp.dot(h_scr[...], wup_ref[...], preferred_element_type=F32)
    a = jnp.square(jnp.maximum(a, 0.0)).astype(BF16)
    acc_scr[...] += jnp.dot(a, wdn_ref[...], preferred_element_type=F32)

    @pl.when(j == pl.num_programs(1) - 1)
    def _():
        y = acc_scr[...]
        if final:
            y = y * lax.rsqrt(jnp.mean(y * y, axis=-1, keepdims=True) + EPS) * gfin_ref[...]
        o_ref[...] = y


def _mlp(x2, g, wup, wdn, gfin, *, tm, tf, final):
    n, d = x2.shape
    dff = wup.shape[1]
    return pl.pallas_call(
        functools.partial(_mlp_kernel, final=final),
        grid=(n // tm, dff // tf),
        in_specs=[pl.BlockSpec((tm, d), lambda i, j: (i, 0)), _resident(g.shape),
                  pl.BlockSpec((d, tf), lambda i, j: (0, j)), pl.BlockSpec((tf, d), lambda i, j: (j, 0)),
                  _resident(gfin.shape)],
        out_specs=pl.BlockSpec((tm, d), lambda i, j: (i, 0)),
        out_shape=jax.ShapeDtypeStruct((n, d), F32),
        scratch_shapes=[pltpu.VMEM((tm, d), BF16), pltpu.VMEM((tm, d), F32)],
        compiler_params=_params(("parallel", "arbitrary")),
        name="mlp",
    )(x2, g, wup, wdn, gfin)


def kernel(x, g_mix, w_in, g_cq, w_uq, w_uq_idx, w_pool, pool_scale, w_o, g_mlp, w_up, w_down, g_final):
    batch, seq, d_model = x.shape
    depth = w_in.shape[0]
    assert TQ == TK and seq % TQ == 0
    cols, col_scale, a_q_rank, b_width, c_width = _in_layout(d_model)
    cos, sin_signed = _rope_tables(seq)
    dil_bias = _dilated_bias(seq // TK)
    groups, gdim = w_pool.shape[1], w_pool.shape[2]
    wcol = jnp.asarray(np.repeat(np.asarray(POOL_WINDOWS, np.float32), gdim)[None, :])
    qa_cols = np.concatenate([_padded_head_cols(h * HEAD_DIM) for h in range(A_HEADS)])
    qi_cols = np.concatenate([_padded_head_cols(h * HEAD_DIM) for h in range(IDX_HEADS)])

    x2 = x.reshape(batch * seq, d_model)
    for l in range(depth):
        w_in_l = (_take_cols(w_in[l], cols) * col_scale[None, :]).astype(BF16)
        wq_l = jnp.concatenate([_take_cols(w_uq[l], qa_cols) * (HEAD_DIM ** -0.5),
                                _take_cols(w_uq_idx[l], qi_cols)], axis=1).astype(BF16)
        q_a, q_idx, kvw, q_b, k_b, v_b, u_c = _in_proj(
            x2, g_mix[l][None, :], w_in_l, g_cq[l][None, :], wq_l, cos, sin_signed,
            batch=batch, seq=seq, a_q_rank=a_q_rank, b_width=b_width, c_width=c_width)
        o_a = _dsa(q_idx, q_a, kvw, batch=batch, seq=seq)
        o_b = _dilated(q_b, k_b, v_b, dil_bias, batch=batch, seq=seq)
        wpool_l = jax.scipy.linalg.block_diag(*[w_pool[l, g] for g in range(groups)]).astype(BF16)
        x2 = _out_proj(x2, o_a, o_b, u_c, wcol, wpool_l, pool_scale[l][None, :], w_o[l].astype(BF16),
                       seq=seq)
        x2 = _mlp(x2, g_mlp[l][None, :], w_up[l].astype(BF16), w_down[l].astype(BF16),
                  g_final[None, :], tm=512, tf=512, final=(l == depth - 1))
    return x2.reshape(batch, seq, d_model)
```

```python
import functools

import numpy as np
import jax
import jax.numpy as jnp
from jax import lax
from jax.experimental import pallas as pl
from jax.experimental.pallas import tpu as pltpu

F32 = jnp.float32
BF16 = jnp.bfloat16

LANES = 128
VMEM_LIMIT = 56 * 1024 * 1024

HEAD_DIM = 64
HALF = HEAD_DIM // 2
ROPE_THETA = 10000.0
EPS = 1e-6
NEG = -1e30
INT_MIN = -(2 ** 31)
A_HEADS = 8
IDX_HEADS = 16
TOPK_MAX = 256
B_HEADS = 12
B_PAIRS = B_HEADS // 2
DILATED_PATTERNS = ((128, 1), (512, 4), (2048, 16))
POOL_WINDOWS = (2, 4, 8, 16)
POOL_HIST = 16

TQ = 256
TK = 256
_NT = (((1,), (1,)), ((), ()))


def _params(sem):
    return pltpu.CompilerParams(dimension_semantics=sem, vmem_limit_bytes=VMEM_LIMIT)


def _resident(shape):
    return pl.BlockSpec(shape, lambda *_: (0,) * len(shape), pipeline_mode=pl.Buffered(1))


def _padded_head_cols(base):
    cols = np.full((LANES,), -1, np.int64)
    cols[0:HALF] = base + np.arange(HALF)
    cols[2 * HALF:3 * HALF] = base + HALF + np.arange(HALF)
    return cols


def _pair_head_cols(base):
    h0, h1 = base, base + HEAD_DIM
    return np.concatenate([h0 + np.arange(HALF), h1 + np.arange(HALF),
                           h0 + HALF + np.arange(HALF), h1 + HALF + np.arange(HALF)])


def _take_cols(w, cols):
    pieces, start = [], 0
    for end in range(1, len(cols) + 1):
        run_ends = end == len(cols) or (cols[end] != cols[end - 1] + 1 if cols[end - 1] >= 0
                                        else cols[end] >= 0)
        if run_ends:
            c0 = int(cols[start])
            pieces.append(jnp.zeros((w.shape[0], end - start), w.dtype) if c0 < 0
                          else w[:, c0:c0 + end - start])
            start = end
    return jnp.concatenate(pieces, axis=1)


def _in_layout(d_model):
    a_q_rank = 3 * d_model // 16
    b_width = B_HEADS * HEAD_DIM
    c_width = 3 * d_model // 8
    sizes = (a_q_rank, HEAD_DIM, HEAD_DIM, HEAD_DIM, IDX_HEADS, b_width, b_width, b_width, c_width)
    off = np.concatenate([[0], np.cumsum(sizes)])
    o_cq, o_ka, o_va, o_ki, o_wi, o_qb, o_kb, o_vb, o_uc = (int(v) for v in off[:-1])
    pad = np.full((LANES,), -1, np.int64)
    va = pad.copy()
    va[:HEAD_DIM] = o_va + np.arange(HEAD_DIM)
    wi = pad.copy()
    wi[:IDX_HEADS] = o_wi + np.arange(IDX_HEADS)
    cols = np.concatenate(
        [o_cq + np.arange(a_q_rank),
         _padded_head_cols(o_ka), _padded_head_cols(o_ki), va, wi]
        + [_pair_head_cols(o_qb + p * LANES) for p in range(B_PAIRS)]
        + [_pair_head_cols(o_kb + p * LANES) for p in range(B_PAIRS)]
        + [o_vb + np.arange(b_width), o_uc + np.arange(c_width)])
    scale = np.ones((cols.shape[0],), np.float32)
    wi_start = a_q_rank + 3 * LANES
    scale[wi_start:wi_start + LANES] = (IDX_HEADS ** -0.5) * (HEAD_DIM ** -0.5)
    qb_start = a_q_rank + 4 * LANES
    scale[qb_start:qb_start + b_width] = HEAD_DIM ** -0.5
    return cols, scale, a_q_rank, b_width, c_width


def _rope_tables(seq):
    inv = ROPE_THETA ** (-jnp.arange(HALF, dtype=F32) / HALF)
    ang = jnp.arange(seq, dtype=F32)[:, None] * inv[None, :]
    cos, sin = jnp.cos(ang), jnp.sin(ang)
    return (jnp.concatenate([cos, cos, cos, cos], axis=1),
            jnp.concatenate([-sin, -sin, sin, sin], axis=1))


def _dilated_bias(nkb):
    d = np.arange(nkb)[:, None, None]
    delta = d * TK + np.arange(TQ)[None, None, :] - np.arange(TK)[None, :, None]
    count = np.zeros(delta.shape, np.float64)
    for window, dil in DILATED_PATTERNS:
        count += (delta >= 0) & (delta % dil == 0) & (delta <= window)
    with np.errstate(divide="ignore"):
        bias = np.where(count > 0, np.log(np.maximum(count, 1.0)), NEG)
    return jnp.asarray(bias, F32)


def _rope_store(dst_ref, x, cos, sin_signed):
    for j in range(x.shape[1] // LANES):
        xb = x[:, j * LANES:(j + 1) * LANES]
        y = (xb * cos + pltpu.roll(xb, 2 * HALF, axis=1) * sin_signed).astype(dst_ref.dtype)
        if len(dst_ref.shape) == 3:
            dst_ref[j] = y
        else:
            dst_ref[:, j * LANES:(j + 1) * LANES] = y


def _in_proj_kernel(x_ref, g_ref, w_ref, gcq_ref, wq_ref, cos_ref, sin_ref,
                    qa_ref, qi_ref, kvw_ref, qb_ref, kb_ref, vb_ref, uc_ref,
                    *, a_q_rank, b_width):
    x = x_ref[...]
    h = (x * lax.rsqrt(jnp.mean(x * x, axis=-1, keepdims=True) + EPS) * g_ref[...]).astype(BF16)
    cos, sin_signed = cos_ref[...], sin_ref[...]

    def proj(start, width):
        return jnp.dot(h, w_ref[:, start:start + width], preferred_element_type=F32)

    cq = proj(0, a_q_rank)
    cqn = (cq * lax.rsqrt(jnp.mean(cq * cq, axis=-1, keepdims=True) + EPS) * gcq_ref[...]).astype(BF16)
    qa_w = A_HEADS * LANES
    _rope_store(qa_ref, jnp.dot(cqn, wq_ref[:, :qa_w], preferred_element_type=F32), cos, sin_signed)
    _rope_store(qi_ref, jnp.dot(cqn, wq_ref[:, qa_w:], preferred_element_type=F32), cos, sin_signed)

    kvw = proj(a_q_rank, 4 * LANES)
    _rope_store(kvw_ref.at[:, :2 * LANES], kvw[:, :2 * LANES], cos, sin_signed)
    kvw_ref[:, 2 * LANES:] = kvw[:, 2 * LANES:].astype(BF16)
    start = a_q_rank + 4 * LANES
    _rope_store(qb_ref, proj(start, b_width), cos, sin_signed)
    _rope_store(kb_ref, proj(start + b_width, b_width), cos, sin_signed)
    vb_ref[...] = proj(start + 2 * b_width, b_width).astype(BF16)
    uc_ref[...] = proj(start + 3 * b_width, uc_ref.shape[1]).astype(BF16)


def _in_proj(x2, g, w, gcq, wq, cos, sin_signed, *, batch, seq, a_q_rank, b_width, c_width):
    n, d = x2.shape
    tm, nq = TQ, seq // TQ
    row = lambda width: pl.BlockSpec((tm, width), lambda i: (i, 0))
    tab = pl.BlockSpec((tm, LANES), lambda i: (i % nq, 0))
    heads = lambda nh: pl.BlockSpec((None, nh, tm, LANES), lambda i: (i // nq, 0, i % nq, 0))
    qshape = lambda nh: jax.ShapeDtypeStruct((batch, nh, seq, LANES), BF16)
    widths = (4 * LANES, b_width, b_width, b_width, c_width)
    return pl.pallas_call(
        functools.partial(_in_proj_kernel, a_q_rank=a_q_rank, b_width=b_width),
        grid=(n // tm,),
        in_specs=[row(d), _resident(g.shape), _resident(w.shape), _resident(gcq.shape),
                  _resident(wq.shape), tab, tab],
        out_specs=[heads(A_HEADS), heads(IDX_HEADS)] + [row(wd) for wd in widths],
        out_shape=[qshape(A_HEADS), qshape(IDX_HEADS)]
                  + [jax.ShapeDtypeStruct((n, wd), BF16) for wd in widths],
        compiler_params=_params(("parallel",)),
        name="in_proj",
    )(x2, g, w, gcq, wq, cos, sin_signed)


def _attend_heads(nheads, q_of, k_of, bias_block, vt_of, raw_scr, acc_scr, ot_scr, last):
    def qk(h, kb, slot):
        raw_scr[slot, h] = lax.dot_general(k_of(h, kb), q_of(h), _NT, preferred_element_type=F32)

    def softmax_pv(h, kb, slot, m_old, bias):
        lg = raw_scr[slot, h] + bias
        m_new = jnp.maximum(m_old, jnp.max(lg, axis=0, keepdims=True))
        p = jnp.exp((lg - m_new).astype(BF16))
        acc_scr[h] = (acc_scr[h] * jnp.exp(m_old - m_new)
                      + jnp.dot(vt_of(h, kb), p, preferred_element_type=F32))
        return m_new

    def block(kb, slot, ms, ahead):
        bias = bias_block(kb)
        new = []
        for h in range(nheads):
            if ahead is not None:
                qk(h, ahead, 1 - slot)
            new.append(softmax_pv(h, kb, slot, ms[h], bias))
        return tuple(new)

    def two_blocks(i, ms):
        kb = 2 * i
        return block(kb + 1, 1, block(kb, 0, ms, kb + 1), kb + 2)

    acc_scr[...] = jnp.zeros(acc_scr.shape, F32)
    for h in range(nheads):
        qk(h, 0, 0)
    pairs = last // 2
    ms = lax.fori_loop(0, pairs, two_blocks, (jnp.full((1, TQ), NEG, F32),) * nheads)

    @pl.when(last % 2 == 0)
    def _():
        block(last, 0, ms, None)

    @pl.when(last % 2 == 1)
    def _():
        block(last, 1, block(last - 1, 0, ms, last), None)

    for h in range(nheads):
        acc = acc_scr[h]
        ot_scr[h * HEAD_DIM:(h + 1) * HEAD_DIM, :] = acc[:HEAD_DIM] * (1.0 / acc[HEAD_DIM:HEAD_DIM + 1])


def _ones_rows_from(vt, first_row):
    row = lax.broadcasted_iota(jnp.int32, vt.shape, 0)
    return jnp.where(row >= first_row, 1.0, vt)


def _store_transposed(o_ref, ot_scr):
    for c in range(ot_scr.shape[0] // LANES):
        o_ref[:, c * LANES:(c + 1) * LANES] = ot_scr[c * LANES:(c + 1) * LANES, :].T.astype(o_ref.dtype)


def _select_threshold(key_scr, hi_scr, lo_scr, ntiles, topk):
    i16_min, i16_max = -(2 ** 15), 2 ** 15 - 1
    slab = 16

    def count_ge(scr, cand):
        c16 = cand.astype(jnp.int16)
        part = jnp.zeros((slab, TQ), jnp.int16)
        for i in range(ntiles):
            m = jnp.where(scr[i] >= c16, jnp.int16(1), jnp.int16(0))
            for r in range(TK // slab):
                part = part + m[r * slab:(r + 1) * slab]
        return jnp.sum(part.astype(jnp.int32), axis=0, keepdims=True)

    def radix16(scr, need, n_all):
        def refine(state, cand):
            thr, n_ge = state
            c = count_ge(scr, cand)
            ok = c >= need
            return jnp.where(ok, cand, thr), jnp.where(ok, c, n_ge)

        state = refine((jnp.full((1, TQ), i16_min, jnp.int32), n_all), jnp.zeros((1, TQ), jnp.int32))
        return lax.fori_loop(
            0, 15, lambda b, st: refine(st, st[0] | jnp.left_shift(jnp.int32(1), 14 - b)), state)

    for i in range(ntiles):
        hi_scr[i] = (key_scr[i] >> 16).astype(jnp.int16)
    n_all = jnp.full((1, TQ), ntiles * TK, jnp.int32)
    t_hi, n_hi = radix16(hi_scr, topk, n_all)
    n_above = jnp.where(t_hi == i16_max, 0, count_ge(hi_scr, jnp.minimum(t_hi + 1, i16_max)))

    t_hi16 = t_hi.astype(jnp.int16)
    for i in range(ntiles):
        lo = ((key_scr[i] & 0xFFFF) + i16_min).astype(jnp.int16)
        lo_scr[i] = jnp.where(hi_scr[i] == t_hi16, lo, jnp.int16(i16_min))
    t_lo, n_lo = radix16(lo_scr, topk - n_above, n_hi - n_above)
    return jnp.left_shift(t_hi, 16) | (t_lo - i16_min), n_above + n_lo


def _dsa_kernel(qi_ref, qa_ref, kvw_ref, o_ref, key_scr, hi_scr, lo_scr, bias_scr, raw_scr, vt_scr,
                acc_scr, ot_scr, thr_scr, cnt_scr, *, topk):
    n = pl.program_id(1)
    nq = key_scr.shape[0]

    @pl.when(n == 0)
    def _():
        for kb in range(nq):
            vt = kvw_ref[kb * TK:(kb + 1) * TK, 2 * LANES:3 * LANES].astype(F32).T
            vt_scr[kb] = _ones_rows_from(vt, HEAD_DIM).astype(BF16)

    def rows(kb, c0):
        return kvw_ref[pl.ds(pl.multiple_of(kb * TK, TK), TK), c0:c0 + LANES]

    w_t = kvw_ref[pl.ds(pl.multiple_of(n * TQ, TQ), TQ), 3 * LANES:4 * LANES].astype(F32).T
    s_loc = lax.broadcasted_iota(jnp.int32, (TK, TQ), 0)
    t_loc = lax.broadcasted_iota(jnp.int32, (TK, TQ), 1)

    def score_block(kb, carry):
        k_idx = rows(kb, LANES)
        acc = jnp.zeros((TK, TQ), F32)
        for h in range(IDX_HEADS):
            d = lax.dot_general(k_idx, qi_ref[h], _NT, preferred_element_type=F32)
            acc = acc + jnp.maximum(d, 0.0) * w_t[h:h + 1, :]
        bits = pltpu.bitcast(acc, jnp.int32)
        key = bits ^ ((bits >> 31) & jnp.int32(0x7FFFFFFF))
        admissible = (kb - n) * TK + s_loc <= t_loc
        key_scr[kb] = jnp.where(admissible, key, jnp.int32(INT_MIN))
        return carry

    lax.fori_loop(0, n + 1, score_block, 0)

    for c in range(nq):
        @pl.when(n == c)
        def _(c=c):
            thr, n_ge = _select_threshold(key_scr, hi_scr, lo_scr, c + 1, topk)
            thr_scr[...] = jnp.broadcast_to(thr, thr_scr.shape)
            cnt_scr[...] = jnp.broadcast_to(n_ge, cnt_scr.shape)

    thr = thr_scr[0:1, :]
    n_ge = cnt_scr[0:1, :]
    floor = jnp.maximum(thr, jnp.int32(INT_MIN + 1))

    def bias_block(kb, carry):
        bias_scr[kb] = jnp.where(key_scr[kb] >= floor, 0.0, NEG)
        return carry

    lax.fori_loop(0, n + 1, bias_block, 0)

    tied = jnp.where(thr != INT_MIN, jnp.where(n_ge > topk, 1.0, 0.0), 0.0)

    @pl.when(jnp.max(tied) > 0.0)
    def _():
        before = (lax.broadcasted_iota(jnp.int32, (TK, TK), 0)
                  > lax.broadcasted_iota(jnp.int32, (TK, TK), 1)).astype(BF16)

        def count_eq(kb, c):
            return c + jnp.sum(jnp.where(key_scr[kb] == thr, 1.0, 0.0), axis=0, keepdims=True)

        n_eq = lax.fori_loop(0, n + 1, count_eq, jnp.zeros((1, TQ), F32))
        need = topk - (n_ge.astype(F32) - n_eq)

        def rebias(kb, seen):
            key = key_scr[kb]
            eq = jnp.where(key == thr, 1.0, 0.0)
            rank = jnp.dot(before, eq.astype(BF16), preferred_element_type=F32) + seen
            keep = jnp.where(key > thr, 1.0, jnp.where(rank < need, eq, 0.0))
            bias_scr[kb] = jnp.where(tied > 0.0, jnp.where(keep > 0.0, 0.0, NEG), bias_scr[kb])
            return seen + jnp.sum(eq, axis=0, keepdims=True)

        lax.fori_loop(0, n + 1, rebias, jnp.zeros((1, TQ), F32))

    _attend_heads(A_HEADS, lambda h: qa_ref[h], lambda h, kb: rows(kb, 0), lambda kb: bias_scr[kb],
                  lambda h, kb: vt_scr[kb], raw_scr, acc_scr, ot_scr, n)
    _store_transposed(o_ref, ot_scr)


def _dsa(q_idx, q_a, kvw, *, batch, seq):
    nq = seq // TQ
    topk = min(TOPK_MAX, seq // 4)
    qblk = lambda nh: pl.BlockSpec((None, nh, TQ, LANES), lambda b, n: (b, 0, n, 0))
    width = A_HEADS * HEAD_DIM
    tile = lambda dtype: pltpu.VMEM((nq, TK, TQ), dtype)
    return pl.pallas_call(
        functools.partial(_dsa_kernel, topk=topk),
        grid=(batch, nq),
        in_specs=[qblk(IDX_HEADS), qblk(A_HEADS), pl.BlockSpec((seq, 4 * LANES), lambda b, n: (b, 0))],
        out_specs=pl.BlockSpec((TQ, width), lambda b, n: (b * nq + n, 0)),
        out_shape=jax.ShapeDtypeStruct((batch * seq, width), BF16),
        scratch_shapes=[tile(jnp.int32), tile(jnp.int16), tile(jnp.int16), tile(F32),
                        pltpu.VMEM((2, A_HEADS, TK, TQ), F32), pltpu.VMEM((nq, LANES, TK), BF16),
                        pltpu.VMEM((A_HEADS, LANES, TQ), F32), pltpu.VMEM((width, TQ), F32),
                        pltpu.VMEM((8, TQ), jnp.int32), pltpu.VMEM((8, TQ), jnp.int32)],
        compiler_params=_params(("parallel", "arbitrary")),
        name="dsa_attention",
    )(q_idx, q_a, kvw)


DIL_PAIRS = 6


def _dilated_kernel(q_ref, k_ref, v_ref, bias_ref, o_ref, qm_scr, raw_scr, vt_scr, acc_scr, ot_scr):
    n = pl.program_id(2)
    nq = vt_scr.shape[1]
    nheads = 2 * DIL_PAIRS

    @pl.when(n == 0)
    def _():
        for p in range(DIL_PAIRS):
            for kb in range(nq):
                vt = v_ref[kb * TK:(kb + 1) * TK, p * LANES:(p + 1) * LANES].astype(F32).T
                vt_scr[2 * p, kb] = _ones_rows_from(vt, HEAD_DIM).astype(BF16)
                vt_scr[2 * p + 1, kb] = jnp.concatenate(
                    [vt[HEAD_DIM:], jnp.ones((LANES - HEAD_DIM, TK), F32)], axis=0).astype(BF16)

    lane = lax.broadcasted_iota(jnp.int32, (TQ, LANES), 1)
    for h in range(nheads):
        q = q_ref[:, (h // 2) * LANES:(h // 2 + 1) * LANES].astype(F32)
        qm_scr[h] = jnp.where((lane // HALF) % 2 == h % 2, q, 0.0).astype(BF16)

    def k_of(h, kb):
        return k_ref[pl.ds(pl.multiple_of(kb * TK, TK), TK), (h // 2) * LANES:(h // 2 + 1) * LANES]

    _attend_heads(nheads, lambda h: qm_scr[h], k_of, lambda kb: bias_ref[n - kb],
                  lambda h, kb: vt_scr[h, kb], raw_scr, acc_scr, ot_scr, n)
    _store_transposed(o_ref, ot_scr)


def _dilated(q_b, k_b, v_b, bias, *, batch, seq):
    nq = seq // TQ
    width = DIL_PAIRS * LANES
    nheads = 2 * DIL_PAIRS
    qblk = pl.BlockSpec((TQ, width), lambda b, g, n: (b * nq + n, g))
    kvblk = pl.BlockSpec((seq, width), lambda b, g, n: (b, g))
    return pl.pallas_call(
        _dilated_kernel,
        grid=(batch, B_PAIRS // DIL_PAIRS, nq),
        in_specs=[qblk, kvblk, kvblk, _resident(bias.shape)],
        out_specs=qblk,
        out_shape=jax.ShapeDtypeStruct(q_b.shape, BF16),
        scratch_shapes=[pltpu.VMEM((nheads, TQ, LANES), BF16), pltpu.VMEM((2, nheads, TK, TQ), F32),
                        pltpu.VMEM((nheads, nq, LANES, TK), BF16), pltpu.VMEM((nheads, LANES, TQ), F32),
                        pltpu.VMEM((nheads * HEAD_DIM, TQ), F32)],
        compiler_params=_params(("parallel", "parallel", "arbitrary")),
        name="dilated_attention",
    )(q_b, k_b, v_b, bias)


def _out_proj_kernel(x_ref, oa_ref, ob_ref, uc_ref, uprev_ref, wcol_ref, wpool_ref, pscale_ref,
                     wo_ref, o_ref, *, seq):
    tm = x_ref.shape[0]
    pos0 = (pl.program_id(0) * tm) % seq
    u = uc_ref[...].astype(F32)
    hist = jnp.where(pos0 == 0, 0.0, uprev_ref[...].astype(F32))
    ext = jnp.concatenate([hist, u], axis=0)
    s2 = ext + pltpu.roll(ext, 1, axis=0)
    s4 = s2 + pltpu.roll(s2, 2, axis=0)
    s8 = s4 + pltpu.roll(s4, 4, axis=0)
    s16 = s8 + pltpu.roll(s8, 8, axis=0)
    wcol = wcol_ref[...]
    sums = jnp.where(wcol == 2.0, s2, jnp.where(wcol == 4.0, s4, jnp.where(wcol == 8.0, s8, s16)))
    pos = (pos0 + lax.broadcasted_iota(jnp.int32, (tm, 1), 0)).astype(F32)
    y = sums[POOL_HIST:] / jnp.minimum(pos + 1.0, wcol) - u
    oc = (jnp.dot(y.astype(BF16), wpool_ref[...], preferred_element_type=F32) * pscale_ref[...]).astype(BF16)
    a_w, b_w = oa_ref.shape[1], ob_ref.shape[1]
    o_ref[...] = (x_ref[...]
                  + jnp.dot(oa_ref[...], wo_ref[:a_w], preferred_element_type=F32)
                  + jnp.dot(ob_ref[...], wo_ref[a_w:a_w + b_w], preferred_element_type=F32)
                  + jnp.dot(oc, wo_ref[a_w + b_w:], preferred_element_type=F32))


def _out_proj(x2, o_a, o_b, u_c, wcol, wpool, pscale, wo, *, seq):
    n, d = x2.shape
    tm = TQ
    row = lambda a: pl.BlockSpec((tm, a.shape[1]), lambda i: (i, 0))
    hist = pl.BlockSpec((POOL_HIST, u_c.shape[1]),
                        lambda i: (jnp.maximum(i * (tm // POOL_HIST) - 1, 0), 0))
    return pl.pallas_call(
        functools.partial(_out_proj_kernel, seq=seq),
        grid=(n // tm,),
        in_specs=[row(x2), row(o_a), row(o_b), row(u_c), hist, _resident(wcol.shape),
                  _resident(wpool.shape), _resident(pscale.shape), _resident(wo.shape)],
        out_specs=row(x2),
        out_shape=jax.ShapeDtypeStruct((n, d), F32),
        compiler_params=_params(("parallel",)),
        name="out_proj",
    )(x2, o_a, o_b, u_c, u_c, wcol, wpool, pscale, wo)


def _mlp_kernel(x_ref, g_ref, wup_ref, wdn_ref, gfin_ref, o_ref, h_scr, *, final):
    j = pl.program_id(1)

    @pl.when(j == 0)
    def _():
        x = x_ref[...]
        h_scr[...] = (x * lax.rsqrt(jnp.mean(x * x, axis=-1, keepdims=True) + EPS) * g_ref[...]).astype(BF16)
        o_ref[...] = x

    a = jnp.dot(h_scr[...], wup_ref[...], preferred_element_type=F32)
    a = jnp.square(jnp.maximum(a, 0.0)).astype(BF16)
    o_ref[...] += jnp.dot(a, wdn_ref[...], preferred_element_type=F32)

    if final:
        @pl.when(j == pl.num_programs(1) - 1)
        def _():
            y = o_ref[...]
            o_ref[...] = y * lax.rsqrt(jnp.mean(y * y, axis=-1, keepdims=True) + EPS) * gfin_ref[...]


def _mlp(x2, g, wup, wdn, gfin, *, tm, tf, final):
    n, d = x2.shape
    dff = wup.shape[1]
    return pl.pallas_call(
        functools.partial(_mlp_kernel, final=final),
        grid=(n // tm, dff // tf),
        in_specs=[pl.BlockSpec((tm, d), lambda i, j: (i, 0)), _resident(g.shape),
                  pl.BlockSpec((d, tf), lambda i, j: (0, j)), pl.BlockSpec((tf, d), lambda i, j: (j, 0)),
                  _resident(gfin.shape)],
        out_specs=pl.BlockSpec((tm, d), lambda i, j: (i, 0)),
        out_shape=jax.ShapeDtypeStruct((n, d), F32),
        scratch_shapes=[pltpu.VMEM((tm, d), BF16)],
        compiler_params=_params(("parallel", "arbitrary")),
        name="mlp",
    )(x2, g, wup, wdn, gfin)


def kernel(x, g_mix, w_in, g_cq, w_uq, w_uq_idx, w_pool, pool_scale, w_o, g_mlp, w_up, w_down, g_final):
    batch, seq, d_model = x.shape
    depth = w_in.shape[0]
    assert TQ == TK and seq % TQ == 0
    cols, col_scale, a_q_rank, b_width, c_width = _in_layout(d_model)
    cos, sin_signed = _rope_tables(seq)
    dil_bias = _dilated_bias(seq // TK)
    groups, gdim = w_pool.shape[1], w_pool.shape[2]
    wcol = jnp.asarray(np.repeat(np.asarray(POOL_WINDOWS, np.float32), gdim)[None, :])
    qa_cols = np.concatenate([_padded_head_cols(h * HEAD_DIM) for h in range(A_HEADS)])
    qi_cols = np.concatenate([_padded_head_cols(h * HEAD_DIM) for h in range(IDX_HEADS)])

    x2 = x.reshape(batch * seq, d_model)
    for l in range(depth):
        w_in_l = (_take_cols(w_in[l], cols) * col_scale[None, :]).astype(BF16)
        wq_l = jnp.concatenate([_take_cols(w_uq[l], qa_cols) * (HEAD_DIM ** -0.5),
                                _take_cols(w_uq_idx[l], qi_cols)], axis=1).astype(BF16)
        q_a, q_idx, kvw, q_b, k_b, v_b, u_c = _in_proj(
            x2, g_mix[l][None, :], w_in_l, g_cq[l][None, :], wq_l, cos, sin_signed,
            batch=batch, seq=seq, a_q_rank=a_q_rank, b_width=b_width, c_width=c_width)
        o_a = _dsa(q_idx, q_a, kvw, batch=batch, seq=seq)
        o_b = _dilated(q_b, k_b, v_b, dil_bias, batch=batch, seq=seq)
        wpool_l = jax.scipy.linalg.block_diag(*[w_pool[l, g] for g in range(groups)]).astype(BF16)
        x2 = _out_proj(x2, o_a, o_b, u_c, wcol, wpool_l, pool_scale[l][None, :], w_o[l].astype(BF16),
                       seq=seq)
        x2 = _mlp(x2, g_mlp[l][None, :], w_up[l].astype(BF16), w_down[l].astype(BF16),
                  g_final[None, :], tm=1024, tf=512, final=(l == depth - 1))
    return x2.reshape(batch, seq, d_model)
```

```python
import functools

import numpy as np
import jax
import jax.numpy as jnp
from jax import lax
from jax.experimental import pallas as pl
from jax.experimental.pallas import tpu as pltpu

F32 = jnp.float32
BF16 = jnp.bfloat16

LANES = 128
VMEM_LIMIT = 56 * 1024 * 1024

HEAD_DIM = 64
HALF = HEAD_DIM // 2
ROPE_THETA = 10000.0
EPS = 1e-6
NEG = -1e30
INT_MIN = -(2 ** 31)
A_HEADS = 8
IDX_HEADS = 16
TOPK_MAX = 256
B_HEADS = 12
B_PAIRS = B_HEADS // 2
DILATED_PATTERNS = ((128, 1), (512, 4), (2048, 16))
POOL_WINDOWS = (2, 4, 8, 16)
POOL_HIST = 16

TQ = 256
TK = 256
_NT = (((1,), (1,)), ((), ()))


def _params(sem):
    return pltpu.CompilerParams(dimension_semantics=sem, vmem_limit_bytes=VMEM_LIMIT)


def _resident(shape):
    return pl.BlockSpec(shape, lambda *_: (0,) * len(shape), pipeline_mode=pl.Buffered(1))


def _layer_resident(stacked, layer):
    rest = stacked.shape[1:]
    return pl.BlockSpec((None, *rest), lambda *_: (layer,) + (0,) * len(rest),
                        pipeline_mode=pl.Buffered(1))


def _padded_head_cols(base):
    cols = np.full((LANES,), -1, np.int64)
    cols[0:HALF] = base + np.arange(HALF)
    cols[2 * HALF:3 * HALF] = base + HALF + np.arange(HALF)
    return cols


def _pair_head_cols(base):
    h0, h1 = base, base + HEAD_DIM
    return np.concatenate([h0 + np.arange(HALF), h1 + np.arange(HALF),
                           h0 + HALF + np.arange(HALF), h1 + HALF + np.arange(HALF)])


def _take_cols(w, cols):
    pieces, start = [], 0
    for end in range(1, len(cols) + 1):
        run_ends = end == len(cols) or (cols[end] != cols[end - 1] + 1 if cols[end - 1] >= 0
                                        else cols[end] >= 0)
        if run_ends:
            c0 = int(cols[start])
            pieces.append(jnp.zeros((*w.shape[:-1], end - start), w.dtype) if c0 < 0
                          else w[..., c0:c0 + end - start])
            start = end
    return jnp.concatenate(pieces, axis=-1)


def _in_layout(d_model):
    a_q_rank = 3 * d_model // 16
    b_width = B_HEADS * HEAD_DIM
    c_width = 3 * d_model // 8
    sizes = (a_q_rank, HEAD_DIM, HEAD_DIM, HEAD_DIM, IDX_HEADS, b_width, b_width, b_width, c_width)
    off = np.concatenate([[0], np.cumsum(sizes)])
    o_cq, o_ka, o_va, o_ki, o_wi, o_qb, o_kb, o_vb, o_uc = (int(v) for v in off[:-1])
    pad = np.full((LANES,), -1, np.int64)
    va = pad.copy()
    va[:HEAD_DIM] = o_va + np.arange(HEAD_DIM)
    wi = pad.copy()
    wi[:IDX_HEADS] = o_wi + np.arange(IDX_HEADS)
    cols = np.concatenate(
        [o_cq + np.arange(a_q_rank),
         _padded_head_cols(o_ka), _padded_head_cols(o_ki), va, wi]
        + [_pair_head_cols(o_qb + p * LANES) for p in range(B_PAIRS)]
        + [_pair_head_cols(o_kb + p * LANES) for p in range(B_PAIRS)]
        + [o_vb + np.arange(b_width), o_uc + np.arange(c_width)])
    scale = np.ones((cols.shape[0],), np.float32)
    wi_start = a_q_rank + 3 * LANES
    scale[wi_start:wi_start + LANES] = (IDX_HEADS ** -0.5) * (HEAD_DIM ** -0.5)
    qb_start = a_q_rank + 4 * LANES
    scale[qb_start:qb_start + b_width] = HEAD_DIM ** -0.5
    return cols, scale, a_q_rank, b_width, c_width


def _rope_tables(seq):
    inv = ROPE_THETA ** (-jnp.arange(HALF, dtype=F32) / HALF)
    ang = jnp.arange(seq, dtype=F32)[:, None] * inv[None, :]
    cos, sin = jnp.cos(ang), jnp.sin(ang)
    return (jnp.concatenate([cos, cos, cos, cos], axis=1),
            jnp.concatenate([-sin, -sin, sin, sin], axis=1))


def _dilated_bias(nkb):
    d = np.arange(nkb)[:, None, None]
    delta = d * TK + np.arange(TQ)[None, None, :] - np.arange(TK)[None, :, None]
    count = np.zeros(delta.shape, np.float64)
    for window, dil in DILATED_PATTERNS:
        count += (delta >= 0) & (delta % dil == 0) & (delta <= window)
    with np.errstate(divide="ignore"):
        bias = np.where(count > 0, np.log(np.maximum(count, 1.0)), NEG)
    return jnp.asarray(bias, F32)


def _rope_store(dst_ref, x, cos, sin_signed):
    for j in range(x.shape[1] // LANES):
        xb = x[:, j * LANES:(j + 1) * LANES]
        y = (xb * cos + pltpu.roll(xb, 2 * HALF, axis=1) * sin_signed).astype(dst_ref.dtype)
        if len(dst_ref.shape) == 3:
            dst_ref[j] = y
        else:
            dst_ref[:, j * LANES:(j + 1) * LANES] = y


def _in_proj_kernel(x_ref, g_ref, w_ref, gcq_ref, wq_ref, cos_ref, sin_ref,
                    qa_ref, qi_ref, kvw_ref, qb_ref, kb_ref, vb_ref, uc_ref,
                    *, a_q_rank, b_width):
    x = x_ref[...]
    h = (x * lax.rsqrt(jnp.mean(x * x, axis=-1, keepdims=True) + EPS) * g_ref[...]).astype(BF16)
    cos, sin_signed = cos_ref[...], sin_ref[...]

    def proj(start, width):
        return jnp.dot(h, w_ref[:, start:start + width], preferred_element_type=F32)

    cq = proj(0, a_q_rank)
    cqn = (cq * lax.rsqrt(jnp.mean(cq * cq, axis=-1, keepdims=True) + EPS) * gcq_ref[...]).astype(BF16)
    qa_w = A_HEADS * LANES
    _rope_store(qa_ref, jnp.dot(cqn, wq_ref[:, :qa_w], preferred_element_type=F32), cos, sin_signed)
    _rope_store(qi_ref, jnp.dot(cqn, wq_ref[:, qa_w:], preferred_element_type=F32), cos, sin_signed)

    kvw = proj(a_q_rank, 4 * LANES)
    _rope_store(kvw_ref.at[:, :2 * LANES], kvw[:, :2 * LANES], cos, sin_signed)
    kvw_ref[:, 2 * LANES:] = kvw[:, 2 * LANES:].astype(BF16)
    start = a_q_rank + 4 * LANES
    _rope_store(qb_ref, proj(start, b_width), cos, sin_signed)
    _rope_store(kb_ref, proj(start + b_width, b_width), cos, sin_signed)
    vb_ref[...] = proj(start + 2 * b_width, b_width).astype(BF16)
    uc_ref[...] = proj(start + 3 * b_width, uc_ref.shape[1]).astype(BF16)


def _in_proj(x2, g, w, gcq, wq, cos, sin_signed, *, layer, batch, seq, a_q_rank, b_width, c_width):
    n, d = x2.shape
    lres = functools.partial(_layer_resident, layer=layer)
    tm = 2 * TQ
    nq = seq // tm
    row = lambda width: pl.BlockSpec((tm, width), lambda i: (i, 0))
    tab = pl.BlockSpec((tm, LANES), lambda i: (i % nq, 0))
    heads = lambda nh: pl.BlockSpec((None, nh, tm, LANES), lambda i: (i // nq, 0, i % nq, 0))
    qshape = lambda nh: jax.ShapeDtypeStruct((batch, nh, seq, LANES), BF16)
    widths = (4 * LANES, b_width, b_width, b_width, c_width)
    return pl.pallas_call(
        functools.partial(_in_proj_kernel, a_q_rank=a_q_rank, b_width=b_width),
        grid=(n // tm,),
        in_specs=[row(d), lres(g), lres(w), lres(gcq), lres(wq), tab, tab],
        out_specs=[heads(A_HEADS), heads(IDX_HEADS)] + [row(wd) for wd in widths],
        out_shape=[qshape(A_HEADS), qshape(IDX_HEADS)]
                  + [jax.ShapeDtypeStruct((n, wd), BF16) for wd in widths],
        compiler_params=_params(("parallel",)),
        name="in_proj",
    )(x2, g, w, gcq, wq, cos, sin_signed)


def _attend_heads(nheads, q_of, k_of, bias_block, vt_of, raw_scr, acc_scr, ot_scr, last):
    def qk(h, kb, slot):
        raw_scr[slot, h] = lax.dot_general(k_of(h, kb), q_of(h), _NT, preferred_element_type=F32)

    def softmax_pv(h, kb, slot, m_old, bias):
        lg = raw_scr[slot, h] + bias
        m_new = jnp.maximum(m_old, jnp.max(lg, axis=0, keepdims=True))
        p = jnp.exp((lg - m_new).astype(BF16))
        acc_scr[h] = (acc_scr[h] * jnp.exp(m_old - m_new)
                      + jnp.dot(vt_of(h, kb), p, preferred_element_type=F32))
        return m_new

    def block(kb, slot, ms, ahead):
        bias = bias_block(kb)
        new = []
        for h in range(nheads):
            if ahead is not None:
                qk(h, ahead, 1 - slot)
            new.append(softmax_pv(h, kb, slot, ms[h], bias))
        return tuple(new)

    def two_blocks(i, ms):
        kb = 2 * i
        return block(kb + 1, 1, block(kb, 0, ms, kb + 1), kb + 2)

    acc_scr[...] = jnp.zeros(acc_scr.shape, F32)
    for h in range(nheads):
        qk(h, 0, 0)
    pairs = last // 2
    ms = lax.fori_loop(0, pairs, two_blocks, (jnp.full((1, TQ), NEG, F32),) * nheads)

    @pl.when(last % 2 == 0)
    def _():
        block(last, 0, ms, None)

    @pl.when(last % 2 == 1)
    def _():
        block(last, 1, block(last - 1, 0, ms, last), None)

    for h in range(nheads):
        acc = acc_scr[h]
        ot_scr[h * HEAD_DIM:(h + 1) * HEAD_DIM, :] = acc[:HEAD_DIM] * (1.0 / acc[HEAD_DIM:HEAD_DIM + 1])


def _ones_rows_from(vt, first_row):
    row = lax.broadcasted_iota(jnp.int32, vt.shape, 0)
    return jnp.where(row >= first_row, 1.0, vt)


def _store_transposed(o_ref, ot_scr):
    for c in range(ot_scr.shape[0] // LANES):
        o_ref[:, c * LANES:(c + 1) * LANES] = ot_scr[c * LANES:(c + 1) * LANES, :].T.astype(o_ref.dtype)


def _select_threshold(key_scr, hi_scr, lo_scr, ntiles, topk):
    i16_min, i16_max = -(2 ** 15), 2 ** 15 - 1
    slab = 16
    chains = 4

    def count_ge(scr, cand):
        c16 = cand.astype(jnp.int16)
        parts = [jnp.zeros((slab, TQ), jnp.int16)] * chains
        for i in range(ntiles):
            m = jnp.where(scr[i] >= c16, jnp.int16(1), jnp.int16(0))
            for r in range(TK // slab):
                parts[r % chains] = parts[r % chains] + m[r * slab:(r + 1) * slab]
        part = functools.reduce(lambda a, b: a + b, parts)
        return jnp.sum(part.astype(jnp.int32), axis=0, keepdims=True)

    def radix16(scr, need, n_all):
        def refine(state, cand):
            thr, n_ge = state
            c = count_ge(scr, cand)
            ok = c >= need
            return jnp.where(ok, cand, thr), jnp.where(ok, c, n_ge)

        state = refine((jnp.full((1, TQ), i16_min, jnp.int32), n_all), jnp.zeros((1, TQ), jnp.int32))
        return lax.fori_loop(
            0, 15, lambda b, st: refine(st, st[0] | jnp.left_shift(jnp.int32(1), 14 - b)), state)

    for i in range(ntiles):
        hi_scr[i] = (key_scr[i] >> 16).astype(jnp.int16)
    n_all = jnp.full((1, TQ), ntiles * TK, jnp.int32)
    t_hi, n_hi = radix16(hi_scr, topk, n_all)
    n_above = jnp.where(t_hi == i16_max, 0, count_ge(hi_scr, jnp.minimum(t_hi + 1, i16_max)))

    t_hi16 = t_hi.astype(jnp.int16)
    for i in range(ntiles):
        lo = ((key_scr[i] & 0xFFFF) + i16_min).astype(jnp.int16)
        lo_scr[i] = jnp.where(hi_scr[i] == t_hi16, lo, jnp.int16(i16_min))
    t_lo, n_lo = radix16(lo_scr, topk - n_above, n_hi - n_above)
    return jnp.left_shift(t_hi, 16) | (t_lo - i16_min), n_above + n_lo


def _dsa_kernel(qi_ref, qa_ref, kvw_ref, o_ref, key_scr, hi_scr, lo_scr, bias_scr, raw_scr, vt_scr,
                acc_scr, ot_scr, thr_scr, cnt_scr, *, topk):
    n = pl.program_id(1)
    nq = key_scr.shape[0]

    @pl.when(n == 0)
    def _():
        for kb in range(nq):
            vt = kvw_ref[kb * TK:(kb + 1) * TK, 2 * LANES:3 * LANES].astype(F32).T
            vt_scr[kb] = _ones_rows_from(vt, HEAD_DIM).astype(BF16)

    def rows(kb, c0):
        return kvw_ref[pl.ds(pl.multiple_of(kb * TK, TK), TK), c0:c0 + LANES]

    w_t = kvw_ref[pl.ds(pl.multiple_of(n * TQ, TQ), TQ), 3 * LANES:4 * LANES].astype(F32).T
    s_loc = lax.broadcasted_iota(jnp.int32, (TK, TQ), 0)
    t_loc = lax.broadcasted_iota(jnp.int32, (TK, TQ), 1)

    def score_block(kb):
        k_idx = rows(kb, LANES)
        acc = jnp.zeros((TK, TQ), F32)
        for h in range(IDX_HEADS):
            d = lax.dot_general(k_idx, qi_ref[h], _NT, preferred_element_type=F32)
            acc = acc + jnp.maximum(d, 0.0) * w_t[h:h + 1, :]
        bits = pltpu.bitcast(acc, jnp.int32)
        key = bits ^ ((bits >> 31) & jnp.int32(0x7FFFFFFF))
        admissible = (kb - n) * TK + s_loc <= t_loc
        key_scr[kb] = jnp.where(admissible, key, jnp.int32(INT_MIN))

    def two_score_blocks(i, carry):
        score_block(2 * i)
        score_block(2 * i + 1)
        return carry

    lax.fori_loop(0, (n + 1) // 2, two_score_blocks, 0)

    @pl.when(n % 2 == 0)
    def _():
        score_block(n)

    for c in range(nq):
        @pl.when(n == c)
        def _(c=c):
            thr, n_ge = _select_threshold(key_scr, hi_scr, lo_scr, c + 1, topk)
            thr_scr[...] = jnp.broadcast_to(thr, thr_scr.shape)
            cnt_scr[...] = jnp.broadcast_to(n_ge, cnt_scr.shape)

    thr = thr_scr[0:1, :]
    n_ge = cnt_scr[0:1, :]
    floor = jnp.maximum(thr, jnp.int32(INT_MIN + 1))

    def bias_block(kb, carry):
        bias_scr[kb] = jnp.where(key_scr[kb] >= floor, 0.0, NEG)
        return carry

    lax.fori_loop(0, n + 1, bias_block, 0)

    tied = jnp.where(thr != INT_MIN, jnp.where(n_ge > topk, 1.0, 0.0), 0.0)

    @pl.when(jnp.max(tied) > 0.0)
    def _():
        before = (lax.broadcasted_iota(jnp.int32, (TK, TK), 0)
                  > lax.broadcasted_iota(jnp.int32, (TK, TK), 1)).astype(BF16)

        def count_eq(kb, c):
            return c + jnp.sum(jnp.where(key_scr[kb] == thr, 1.0, 0.0), axis=0, keepdims=True)

        n_eq = lax.fori_loop(0, n + 1, count_eq, jnp.zeros((1, TQ), F32))
        need = topk - (n_ge.astype(F32) - n_eq)

        def rebias(kb, seen):
            key = key_scr[kb]
            eq = jnp.where(key == thr, 1.0, 0.0)
            rank = jnp.dot(before, eq.astype(BF16), preferred_element_type=F32) + seen
            keep = jnp.where(key > thr, 1.0, jnp.where(rank < need, eq, 0.0))
            bias_scr[kb] = jnp.where(tied > 0.0, jnp.where(keep > 0.0, 0.0, NEG), bias_scr[kb])
            return seen + jnp.sum(eq, axis=0, keepdims=True)

        lax.fori_loop(0, n + 1, rebias, jnp.zeros((1, TQ), F32))

    _attend_heads(A_HEADS, lambda h: qa_ref[h], lambda h, kb: rows(kb, 0), lambda kb: bias_scr[kb],
                  lambda h, kb: vt_scr[kb], raw_scr, acc_scr, ot_scr, n)
    _store_transposed(o_ref, ot_scr)


def _dsa(q_idx, q_a, kvw, *, batch, seq):
    nq = seq // TQ
    topk = min(TOPK_MAX, seq // 4)
    qblk = lambda nh: pl.BlockSpec((None, nh, TQ, LANES), lambda b, n: (b, 0, n, 0))
    width = A_HEADS * HEAD_DIM
    tile = lambda dtype: pltpu.VMEM((nq, TK, TQ), dtype)
    return pl.pallas_call(
        functools.partial(_dsa_kernel, topk=topk),
        grid=(batch, nq),
        in_specs=[qblk(IDX_HEADS), qblk(A_HEADS), pl.BlockSpec((seq, 4 * LANES), lambda b, n: (b, 0))],
        out_specs=pl.BlockSpec((TQ, width), lambda b, n: (b * nq + n, 0)),
        out_shape=jax.ShapeDtypeStruct((batch * seq, width), BF16),
        scratch_shapes=[tile(jnp.int32), tile(jnp.int16), tile(jnp.int16), tile(F32),
                        pltpu.VMEM((2, A_HEADS, TK, TQ), F32), pltpu.VMEM((nq, LANES, TK), BF16),
                        pltpu.VMEM((A_HEADS, LANES, TQ), F32), pltpu.VMEM((width, TQ), F32),
                        pltpu.VMEM((8, TQ), jnp.int32), pltpu.VMEM((8, TQ), jnp.int32)],
        compiler_params=_params(("parallel", "arbitrary")),
        name="dsa_attention",
    )(q_idx, q_a, kvw)


DIL_PAIRS = 6


def _dilated_kernel(q_ref, k_ref, v_ref, bias_ref, o_ref, qm_scr, raw_scr, vt_scr, acc_scr, ot_scr):
    n = pl.program_id(2)
    nq = vt_scr.shape[1]
    nheads = 2 * DIL_PAIRS

    @pl.when(n == 0)
    def _():
        for p in range(DIL_PAIRS):
            for kb in range(nq):
                vt = v_ref[kb * TK:(kb + 1) * TK, p * LANES:(p + 1) * LANES].astype(F32).T
                vt_scr[2 * p, kb] = _ones_rows_from(vt, HEAD_DIM).astype(BF16)
                vt_scr[2 * p + 1, kb] = jnp.concatenate(
                    [vt[HEAD_DIM:], jnp.ones((LANES - HEAD_DIM, TK), F32)], axis=0).astype(BF16)

    lane = lax.broadcasted_iota(jnp.int32, (TQ, LANES), 1)
    for h in range(nheads):
        q = q_ref[:, (h // 2) * LANES:(h // 2 + 1) * LANES].astype(F32)
        qm_scr[h] = jnp.where((lane // HALF) % 2 == h % 2, q, 0.0).astype(BF16)

    def k_of(h, kb):
        return k_ref[pl.ds(pl.multiple_of(kb * TK, TK), TK), (h // 2) * LANES:(h // 2 + 1) * LANES]

    _attend_heads(nheads, lambda h: qm_scr[h], k_of, lambda kb: bias_ref[n - kb],
                  lambda h, kb: vt_scr[h, kb], raw_scr, acc_scr, ot_scr, n)
    _store_transposed(o_ref, ot_scr)


def _dilated(q_b, k_b, v_b, bias, *, batch, seq):
    nq = seq // TQ
    width = DIL_PAIRS * LANES
    nheads = 2 * DIL_PAIRS
    qblk = pl.BlockSpec((TQ, width), lambda b, g, n: (b * nq + n, g))
    kvblk = pl.BlockSpec((seq, width), lambda b, g, n: (b, g))
    return pl.pallas_call(
        _dilated_kernel,
        grid=(batch, B_PAIRS // DIL_PAIRS, nq),
        in_specs=[qblk, kvblk, kvblk, _resident(bias.shape)],
        out_specs=qblk,
        out_shape=jax.ShapeDtypeStruct(q_b.shape, BF16),
        scratch_shapes=[pltpu.VMEM((nheads, TQ, LANES), BF16), pltpu.VMEM((2, nheads, TK, TQ), F32),
                        pltpu.VMEM((nheads, nq, LANES, TK), BF16), pltpu.VMEM((nheads, LANES, TQ), F32),
                        pltpu.VMEM((nheads * HEAD_DIM, TQ), F32)],
        compiler_params=_params(("parallel", "parallel", "arbitrary")),
        name="dilated_attention",
    )(q_b, k_b, v_b, bias)


def _out_proj_kernel(x_ref, oa_ref, ob_ref, uc_ref, uprev_ref, wcol_ref, wpool_ref, pscale_ref,
                     wo_ref, o_ref, *, seq):
    tm = x_ref.shape[0]
    pos0 = (pl.program_id(0) * tm) % seq
    u = uc_ref[...].astype(F32)
    hist = jnp.where(pos0 == 0, 0.0, uprev_ref[...].astype(F32))
    ext = jnp.concatenate([hist, u], axis=0)
    s2 = ext + pltpu.roll(ext, 1, axis=0)
    s4 = s2 + pltpu.roll(s2, 2, axis=0)
    s8 = s4 + pltpu.roll(s4, 4, axis=0)
    s16 = s8 + pltpu.roll(s8, 8, axis=0)
    wcol = wcol_ref[...]
    sums = jnp.where(wcol == 2.0, s2, jnp.where(wcol == 4.0, s4, jnp.where(wcol == 8.0, s8, s16)))
    pos = (pos0 + lax.broadcasted_iota(jnp.int32, (tm, 1), 0)).astype(F32)
    y = sums[POOL_HIST:] / jnp.minimum(pos + 1.0, wcol) - u
    oc = (jnp.dot(y.astype(BF16), wpool_ref[...], preferred_element_type=F32) * pscale_ref[...]).astype(BF16)
    a_w, b_w = oa_ref.shape[1], ob_ref.shape[1]
    o_ref[...] = (x_ref[...]
                  + jnp.dot(oa_ref[...], wo_ref[:a_w], preferred_element_type=F32)
                  + jnp.dot(ob_ref[...], wo_ref[a_w:a_w + b_w], preferred_element_type=F32)
                  + jnp.dot(oc, wo_ref[a_w + b_w:], preferred_element_type=F32))


def _out_proj(x2, o_a, o_b, u_c, wcol, wpool, pscale, wo, *, layer, seq):
    n, d = x2.shape
    tm = 2 * TQ
    lres = functools.partial(_layer_resident, layer=layer)
    row = lambda a: pl.BlockSpec((tm, a.shape[1]), lambda i: (i, 0))
    hist = pl.BlockSpec((POOL_HIST, u_c.shape[1]),
                        lambda i: (jnp.maximum(i * (tm // POOL_HIST) - 1, 0), 0))
    return pl.pallas_call(
        functools.partial(_out_proj_kernel, seq=seq),
        grid=(n // tm,),
        in_specs=[row(x2), row(o_a), row(o_b), row(u_c), hist, _resident(wcol.shape),
                  lres(wpool), lres(pscale), lres(wo)],
        out_specs=row(x2),
        out_shape=jax.ShapeDtypeStruct((n, d), F32),
        compiler_params=_params(("parallel",)),
        name="out_proj",
    )(x2, o_a, o_b, u_c, u_c, wcol, wpool, pscale, wo)


def _mlp_kernel(x_ref, g_ref, wup_ref, wdn_ref, gfin_ref, o_ref, h_scr, *, final):
    j = pl.program_id(1)

    @pl.when(j == 0)
    def _():
        x = x_ref[...]
        h_scr[...] = (x * lax.rsqrt(jnp.mean(x * x, axis=-1, keepdims=True) + EPS) * g_ref[...]).astype(BF16)
        o_ref[...] = x

    a = jnp.dot(h_scr[...], wup_ref[...], preferred_element_type=F32)
    a = jnp.square(jnp.maximum(a, 0.0)).astype(BF16)
    o_ref[...] += jnp.dot(a, wdn_ref[...], preferred_element_type=F32)

    if final:
        @pl.when(j == pl.num_programs(1) - 1)
        def _():
            y = o_ref[...]
            o_ref[...] = y * lax.rsqrt(jnp.mean(y * y, axis=-1, keepdims=True) + EPS) * gfin_ref[...]


def _mlp(x2, g, wup, wdn, gfin, *, layer, tm, tf, final):
    n, d = x2.shape
    dff = wup.shape[-1]
    return pl.pallas_call(
        functools.partial(_mlp_kernel, final=final),
        grid=(n // tm, dff // tf),
        in_specs=[pl.BlockSpec((tm, d), lambda i, j: (i, 0)), _layer_resident(g, layer),
                  pl.BlockSpec((None, d, tf), lambda i, j: (layer, 0, j)),
                  pl.BlockSpec((None, tf, d), lambda i, j: (layer, j, 0)),
                  _resident(gfin.shape)],
        out_specs=pl.BlockSpec((tm, d), lambda i, j: (i, 0)),
        out_shape=jax.ShapeDtypeStruct((n, d), F32),
        scratch_shapes=[pltpu.VMEM((tm, d), BF16)],
        compiler_params=_params(("parallel", "arbitrary")),
        name="mlp",
    )(x2, g, wup, wdn, gfin)


def kernel(x, g_mix, w_in, g_cq, w_uq, w_uq_idx, w_pool, pool_scale, w_o, g_mlp, w_up, w_down, g_final):
    batch, seq, d_model = x.shape
    depth = w_in.shape[0]
    assert TQ == TK and seq % (2 * TQ) == 0
    cols, col_scale, a_q_rank, b_width, c_width = _in_layout(d_model)
    cos, sin_signed = _rope_tables(seq)
    dil_bias = _dilated_bias(seq // TK)
    groups, gdim = w_pool.shape[1], w_pool.shape[2]
    wcol = jnp.asarray(np.repeat(np.asarray(POOL_WINDOWS, np.float32), gdim)[None, :])
    qa_cols = np.concatenate([_padded_head_cols(h * HEAD_DIM) for h in range(A_HEADS)])
    qi_cols = np.concatenate([_padded_head_cols(h * HEAD_DIM) for h in range(IDX_HEADS)])

    w_in_p = (_take_cols(w_in, cols) * col_scale).astype(BF16)
    wq_p = jnp.concatenate([_take_cols(w_uq, qa_cols) * (HEAD_DIM ** -0.5),
                            _take_cols(w_uq_idx, qi_cols)], axis=-1).astype(BF16)
    wpool_p = jnp.stack([jax.scipy.linalg.block_diag(*[w_pool[l, g] for g in range(groups)])
                         for l in range(depth)]).astype(BF16)
    wo_p, wup_p, wdn_p = w_o.astype(BF16), w_up.astype(BF16), w_down.astype(BF16)
    g_mix_p, g_cq_p, g_mlp_p, pscale_p = (a[:, None, :] for a in (g_mix, g_cq, g_mlp, pool_scale))

    x2 = x.reshape(batch * seq, d_model)
    for l in range(depth):
        q_a, q_idx, kvw, q_b, k_b, v_b, u_c = _in_proj(
            x2, g_mix_p, w_in_p, g_cq_p, wq_p, cos, sin_signed, layer=l,
            batch=batch, seq=seq, a_q_rank=a_q_rank, b_width=b_width, c_width=c_width)
        o_a = _dsa(q_idx, q_a, kvw, batch=batch, seq=seq)
        o_b = _dilated(q_b, k_b, v_b, dil_bias, batch=batch, seq=seq)
        x2 = _out_proj(x2, o_a, o_b, u_c, wcol, wpool_p, pscale_p, wo_p, layer=l, seq=seq)
        x2 = _mlp(x2, g_mlp_p, wup_p, wdn_p, g_final[None, :], layer=l, tm=1024, tf=512,
                  final=(l == depth - 1))
    return x2.reshape(batch, seq, d_model)
```

```python
import functools

import numpy as np
import jax
import jax.numpy as jnp
from jax import lax
from jax.experimental import pallas as pl
from jax.experimental.pallas import tpu as pltpu

F32 = jnp.float32
BF16 = jnp.bfloat16

LANES = 128
VMEM_LIMIT = 56 * 1024 * 1024

HEAD_DIM = 64
HALF = HEAD_DIM // 2
ROPE_THETA = 10000.0
EPS = 1e-6
NEG = -(2.0 ** 100)
INT_MIN = -(2 ** 31)
A_HEADS = 8
IDX_HEADS = 16
TOPK_MAX = 256
B_HEADS = 12
B_PAIRS = B_HEADS // 2
DILATED_PATTERNS = ((128, 1), (512, 4), (2048, 16))
POOL_WINDOWS = (2, 4, 8, 16)
POOL_HIST = 16

TQ = 256
TK = 256
_NT = (((1,), (1,)), ((), ()))


def _params(sem):
    return pltpu.CompilerParams(dimension_semantics=sem, vmem_limit_bytes=VMEM_LIMIT)


def _resident(shape):
    return pl.BlockSpec(shape, lambda *_: (0,) * len(shape), pipeline_mode=pl.Buffered(1))


def _layer_resident(stacked, layer):
    rest = stacked.shape[1:]
    return pl.BlockSpec((None, *rest), lambda *_: (layer,) + (0,) * len(rest),
                        pipeline_mode=pl.Buffered(1))


def _padded_head_cols(base):
    cols = np.full((LANES,), -1, np.int64)
    cols[0:HALF] = base + np.arange(HALF)
    cols[2 * HALF:3 * HALF] = base + HALF + np.arange(HALF)
    return cols


def _pair_head_cols(base):
    h0, h1 = base, base + HEAD_DIM
    return np.concatenate([h0 + np.arange(HALF), h1 + np.arange(HALF),
                           h0 + HALF + np.arange(HALF), h1 + HALF + np.arange(HALF)])


def _take_cols(w, cols):
    pieces, start = [], 0
    for end in range(1, len(cols) + 1):
        run_ends = end == len(cols) or (cols[end] != cols[end - 1] + 1 if cols[end - 1] >= 0
                                        else cols[end] >= 0)
        if run_ends:
            c0 = int(cols[start])
            pieces.append(jnp.zeros((*w.shape[:-1], end - start), w.dtype) if c0 < 0
                          else w[..., c0:c0 + end - start])
            start = end
    return jnp.concatenate(pieces, axis=-1)


def _in_layout(d_model):
    a_q_rank = 3 * d_model // 16
    b_width = B_HEADS * HEAD_DIM
    c_width = 3 * d_model // 8
    sizes = (a_q_rank, HEAD_DIM, HEAD_DIM, HEAD_DIM, IDX_HEADS, b_width, b_width, b_width, c_width)
    off = np.concatenate([[0], np.cumsum(sizes)])
    o_cq, o_ka, o_va, o_ki, o_wi, o_qb, o_kb, o_vb, o_uc = (int(v) for v in off[:-1])
    pad = np.full((LANES,), -1, np.int64)
    va = pad.copy()
    va[:HEAD_DIM] = o_va + np.arange(HEAD_DIM)
    wi = pad.copy()
    wi[:IDX_HEADS] = o_wi + np.arange(IDX_HEADS)
    cols = np.concatenate(
        [o_cq + np.arange(a_q_rank),
         _padded_head_cols(o_ka), _padded_head_cols(o_ki), va, wi]
        + [_pair_head_cols(o_qb + p * LANES) for p in range(B_PAIRS)]
        + [_pair_head_cols(o_kb + p * LANES) for p in range(B_PAIRS)]
        + [o_vb + np.arange(b_width), o_uc + np.arange(c_width)])
    scale = np.ones((cols.shape[0],), np.float32)
    wi_start = a_q_rank + 3 * LANES
    scale[wi_start:wi_start + LANES] = (IDX_HEADS ** -0.5) * (HEAD_DIM ** -0.5)
    qb_start = a_q_rank + 4 * LANES
    scale[qb_start:qb_start + b_width] = HEAD_DIM ** -0.5
    return cols, scale, a_q_rank, b_width, c_width


def _rope_tables(seq):
    inv = ROPE_THETA ** (-jnp.arange(HALF, dtype=F32) / HALF)
    ang = jnp.arange(seq, dtype=F32)[:, None] * inv[None, :]
    cos, sin = jnp.cos(ang), jnp.sin(ang)
    return (jnp.concatenate([cos, cos, cos, cos], axis=1),
            jnp.concatenate([-sin, -sin, sin, sin], axis=1))


def _dilated_bias(nkb):
    d = np.arange(nkb)[:, None, None]
    delta = d * TK + np.arange(TQ)[None, None, :] - np.arange(TK)[None, :, None]
    count = np.zeros(delta.shape, np.float64)
    for window, dil in DILATED_PATTERNS:
        count += (delta >= 0) & (delta % dil == 0) & (delta <= window)
    with np.errstate(divide="ignore"):
        bias = np.where(count > 0, np.log(np.maximum(count, 1.0)), NEG)
    return jnp.asarray(bias, F32)


def _rope_store(dst_ref, x, cos, sin_signed):
    for j in range(x.shape[1] // LANES):
        xb = x[:, j * LANES:(j + 1) * LANES]
        y = (xb * cos + pltpu.roll(xb, 2 * HALF, axis=1) * sin_signed).astype(dst_ref.dtype)
        if len(dst_ref.shape) == 3:
            dst_ref[j] = y
        else:
            dst_ref[:, j * LANES:(j + 1) * LANES] = y


def _in_proj_kernel(x_ref, g_ref, w_ref, gcq_ref, wq_ref, cos_ref, sin_ref,
                    qa_ref, qi_ref, kvw_ref, qb_ref, kb_ref, vb_ref, uc_ref,
                    *, a_q_rank, b_width):
    x = x_ref[...]
    h = (x * lax.rsqrt(jnp.mean(x * x, axis=-1, keepdims=True) + EPS) * g_ref[...]).astype(BF16)
    cos, sin_signed = cos_ref[...], sin_ref[...]

    def proj(start, width):
        return jnp.dot(h, w_ref[:, start:start + width], preferred_element_type=F32)

    cq = proj(0, a_q_rank)
    cqn = (cq * lax.rsqrt(jnp.mean(cq * cq, axis=-1, keepdims=True) + EPS) * gcq_ref[...]).astype(BF16)
    qa_w = A_HEADS * LANES
    _rope_store(qa_ref, jnp.dot(cqn, wq_ref[:, :qa_w], preferred_element_type=F32), cos, sin_signed)
    _rope_store(qi_ref, jnp.dot(cqn, wq_ref[:, qa_w:], preferred_element_type=F32), cos, sin_signed)

    kvw = proj(a_q_rank, 4 * LANES)
    _rope_store(kvw_ref.at[:, :2 * LANES], kvw[:, :2 * LANES], cos, sin_signed)
    kvw_ref[:, 2 * LANES:] = kvw[:, 2 * LANES:].astype(BF16)
    start = a_q_rank + 4 * LANES
    _rope_store(qb_ref, proj(start, b_width), cos, sin_signed)
    _rope_store(kb_ref, proj(start + b_width, b_width), cos, sin_signed)
    vb_ref[...] = proj(start + 2 * b_width, b_width).astype(BF16)
    uc_ref[...] = proj(start + 3 * b_width, uc_ref.shape[1]).astype(BF16)


def _in_proj(x2, g, w, gcq, wq, cos, sin_signed, *, layer, batch, seq, a_q_rank, b_width, c_width):
    n, d = x2.shape
    lres = functools.partial(_layer_resident, layer=layer)
    tm = 2 * TQ
    nq = seq // tm
    row = lambda width: pl.BlockSpec((tm, width), lambda i: (i, 0))
    tab = pl.BlockSpec((tm, LANES), lambda i: (i % nq, 0))
    heads = lambda nh: pl.BlockSpec((None, nh, tm, LANES), lambda i: (i // nq, 0, i % nq, 0))
    qshape = lambda nh: jax.ShapeDtypeStruct((batch, nh, seq, LANES), BF16)
    widths = (4 * LANES, b_width, b_width, b_width, c_width)
    return pl.pallas_call(
        functools.partial(_in_proj_kernel, a_q_rank=a_q_rank, b_width=b_width),
        grid=(n // tm,),
        in_specs=[row(d), lres(g), lres(w), lres(gcq), lres(wq), tab, tab],
        out_specs=[heads(A_HEADS), heads(IDX_HEADS)] + [row(wd) for wd in widths],
        out_shape=[qshape(A_HEADS), qshape(IDX_HEADS)]
                  + [jax.ShapeDtypeStruct((n, wd), BF16) for wd in widths],
        compiler_params=_params(("parallel",)),
        name="in_proj",
    )(x2, g, w, gcq, wq, cos, sin_signed)


def _attend_heads(nheads, q_of, k_of, bias_block, vt_of, raw_scr, acc_scr, ot_scr, last):
    def qk(h, kb, slot):
        raw_scr[slot, h] = lax.dot_general(k_of(h, kb), q_of(h), _NT, preferred_element_type=F32)

    def softmax_pv(h, kb, slot, m_old, bias):
        lg = (raw_scr[slot, h] + bias).astype(BF16)
        m_new = jnp.maximum(m_old, jnp.max(lg, axis=0, keepdims=True).astype(F32))
        p = jnp.exp(lg - m_new.astype(BF16))
        acc_scr[h] = (acc_scr[h] * jnp.exp(m_old - m_new)
                      + jnp.dot(vt_of(h, kb), p, preferred_element_type=F32))
        return m_new

    def block(kb, slot, ms, ahead):
        bias = bias_block(kb)
        new = []
        for h in range(nheads):
            if ahead is not None:
                qk(h, ahead, 1 - slot)
            new.append(softmax_pv(h, kb, slot, ms[h], bias))
        return tuple(new)

    def two_blocks(i, ms):
        kb = 2 * i
        return block(kb + 1, 1, block(kb, 0, ms, kb + 1), kb + 2)

    acc_scr[...] = jnp.zeros(acc_scr.shape, F32)
    for h in range(nheads):
        qk(h, 0, 0)
    pairs = last // 2
    ms = lax.fori_loop(0, pairs, two_blocks, (jnp.full((1, TQ), NEG, F32),) * nheads)

    @pl.when(last % 2 == 0)
    def _():
        block(last, 0, ms, None)

    @pl.when(last % 2 == 1)
    def _():
        block(last, 1, block(last - 1, 0, ms, last), None)

    for h in range(nheads):
        acc = acc_scr[h]
        ot_scr[h * HEAD_DIM:(h + 1) * HEAD_DIM, :] = acc[:HEAD_DIM] * (1.0 / acc[HEAD_DIM:HEAD_DIM + 1])


def _ones_rows_from(vt, first_row):
    row = lax.broadcasted_iota(jnp.int32, vt.shape, 0)
    return jnp.where(row >= first_row, 1.0, vt)


def _store_transposed(o_ref, ot_scr):
    for c in range(ot_scr.shape[0] // LANES):
        o_ref[:, c * LANES:(c + 1) * LANES] = ot_scr[c * LANES:(c + 1) * LANES, :].T.astype(o_ref.dtype)


def _select_threshold(key_scr, hi_scr, lo_scr, ntiles, topk):
    i16_min, i16_max = -(2 ** 15), 2 ** 15 - 1
    slab = 16
    chains = 4

    def count_ge(scr, cand):
        c16 = cand.astype(jnp.int16)
        parts = [jnp.zeros((slab, TQ), jnp.int16)] * chains
        for i in range(ntiles):
            m = jnp.where(scr[i] >= c16, jnp.int16(1), jnp.int16(0))
            for r in range(TK // slab):
                parts[r % chains] = parts[r % chains] + m[r * slab:(r + 1) * slab]
        part = functools.reduce(lambda a, b: a + b, parts)
        return jnp.sum(part.astype(jnp.int32), axis=0, keepdims=True)

    def radix16(scr, need, n_all):
        def refine(state, cand):
            thr, n_ge = state
            c = count_ge(scr, cand)
            ok = c >= need
            return jnp.where(ok, cand, thr), jnp.where(ok, c, n_ge)

        state = refine((jnp.full((1, TQ), i16_min, jnp.int32), n_all), jnp.zeros((1, TQ), jnp.int32))
        return lax.fori_loop(
            0, 15, lambda b, st: refine(st, st[0] | jnp.left_shift(jnp.int32(1), 14 - b)), state)

    for i in range(ntiles):
        hi_scr[i] = (key_scr[i] >> 16).astype(jnp.int16)
    n_all = jnp.full((1, TQ), ntiles * TK, jnp.int32)
    t_hi, n_hi = radix16(hi_scr, topk, n_all)
    n_above = jnp.where(t_hi == i16_max, 0, count_ge(hi_scr, jnp.minimum(t_hi + 1, i16_max)))

    t_hi16 = t_hi.astype(jnp.int16)
    for i in range(ntiles):
        lo = ((key_scr[i] & 0xFFFF) + i16_min).astype(jnp.int16)
        lo_scr[i] = jnp.where(hi_scr[i] == t_hi16, lo, jnp.int16(i16_min))
    t_lo, n_lo = radix16(lo_scr, topk - n_above, n_hi - n_above)
    return jnp.left_shift(t_hi, 16) | (t_lo - i16_min), n_above + n_lo


def _dsa_kernel(qi_ref, qa_ref, kvw_ref, o_ref, key_scr, hi_scr, lo_scr, bias_scr, raw_scr, vt_scr,
                acc_scr, ot_scr, thr_scr, cnt_scr, *, topk):
    n = pl.program_id(1)
    nq = key_scr.shape[0]

    @pl.when(n == 0)
    def _():
        for kb in range(nq):
            vt = kvw_ref[kb * TK:(kb + 1) * TK, 2 * LANES:3 * LANES].astype(F32).T
            vt_scr[kb] = _ones_rows_from(vt, HEAD_DIM).astype(BF16)

    def rows(kb, c0):
        return kvw_ref[pl.ds(pl.multiple_of(kb * TK, TK), TK), c0:c0 + LANES]

    w_t = kvw_ref[pl.ds(pl.multiple_of(n * TQ, TQ), TQ), 3 * LANES:4 * LANES].astype(F32).T
    s_loc = lax.broadcasted_iota(jnp.int32, (TK, TQ), 0)
    t_loc = lax.broadcasted_iota(jnp.int32, (TK, TQ), 1)

    def score_block(kb):
        k_idx = rows(kb, LANES)
        acc = jnp.zeros((TK, TQ), F32)
        for h in range(IDX_HEADS):
            d = lax.dot_general(k_idx, qi_ref[h], _NT, preferred_element_type=F32)
            acc = acc + jnp.maximum(d, 0.0) * w_t[h:h + 1, :]
        bits = pltpu.bitcast(acc, jnp.int32)
        key = bits ^ ((bits >> 31) & jnp.int32(0x7FFFFFFF))
        admissible = (kb - n) * TK + s_loc <= t_loc
        key_scr[kb] = jnp.where(admissible, key, jnp.int32(INT_MIN))

    def two_score_blocks(i, carry):
        score_block(2 * i)
        score_block(2 * i + 1)
        return carry

    lax.fori_loop(0, (n + 1) // 2, two_score_blocks, 0)

    @pl.when(n % 2 == 0)
    def _():
        score_block(n)

    for c in range(nq):
        @pl.when(n == c)
        def _(c=c):
            thr, n_ge = _select_threshold(key_scr, hi_scr, lo_scr, c + 1, topk)
            thr_scr[...] = jnp.broadcast_to(thr, thr_scr.shape)
            cnt_scr[...] = jnp.broadcast_to(n_ge, cnt_scr.shape)

    thr = thr_scr[0:1, :]
    n_ge = cnt_scr[0:1, :]
    floor = jnp.maximum(thr, jnp.int32(INT_MIN + 1))

    def bias_block(kb, carry):
        bias_scr[kb] = jnp.where(key_scr[kb] >= floor, 0.0, NEG)
        return carry

    lax.fori_loop(0, n + 1, bias_block, 0)

    tied = jnp.where(thr != INT_MIN, jnp.where(n_ge > topk, 1.0, 0.0), 0.0)

    @pl.when(jnp.max(tied) > 0.0)
    def _():
        before = (lax.broadcasted_iota(jnp.int32, (TK, TK), 0)
                  > lax.broadcasted_iota(jnp.int32, (TK, TK), 1)).astype(BF16)

        def count_eq(kb, c):
            return c + jnp.sum(jnp.where(key_scr[kb] == thr, 1.0, 0.0), axis=0, keepdims=True)

        n_eq = lax.fori_loop(0, n + 1, count_eq, jnp.zeros((1, TQ), F32))
        need = topk - (n_ge.astype(F32) - n_eq)

        def rebias(kb, seen):
            key = key_scr[kb]
            eq = jnp.where(key == thr, 1.0, 0.0)
            rank = jnp.dot(before, eq.astype(BF16), preferred_element_type=F32) + seen
            keep = jnp.where(key > thr, 1.0, jnp.where(rank < need, eq, 0.0))
            bias_scr[kb] = jnp.where(tied > 0.0, jnp.where(keep > 0.0, 0.0, NEG), bias_scr[kb])
            return seen + jnp.sum(eq, axis=0, keepdims=True)

        lax.fori_loop(0, n + 1, rebias, jnp.zeros((1, TQ), F32))

    _attend_heads(A_HEADS, lambda h: qa_ref[h], lambda h, kb: rows(kb, 0), lambda kb: bias_scr[kb],
                  lambda h, kb: vt_scr[kb], raw_scr, acc_scr, ot_scr, n)
    _store_transposed(o_ref, ot_scr)


def _dsa(q_idx, q_a, kvw, *, batch, seq):
    nq = seq // TQ
    topk = min(TOPK_MAX, seq // 4)
    qblk = lambda nh: pl.BlockSpec((None, nh, TQ, LANES), lambda b, n: (b, 0, n, 0))
    width = A_HEADS * HEAD_DIM
    tile = lambda dtype: pltpu.VMEM((nq, TK, TQ), dtype)
    return pl.pallas_call(
        functools.partial(_dsa_kernel, topk=topk),
        grid=(batch, nq),
        in_specs=[qblk(IDX_HEADS), qblk(A_HEADS), pl.BlockSpec((seq, 4 * LANES), lambda b, n: (b, 0))],
        out_specs=pl.BlockSpec((TQ, width), lambda b, n: (b * nq + n, 0)),
        out_shape=jax.ShapeDtypeStruct((batch * seq, width), BF16),
        scratch_shapes=[tile(jnp.int32), tile(jnp.int16), tile(jnp.int16), tile(F32),
                        pltpu.VMEM((2, A_HEADS, TK, TQ), F32), pltpu.VMEM((nq, LANES, TK), BF16),
                        pltpu.VMEM((A_HEADS, LANES, TQ), F32), pltpu.VMEM((width, TQ), F32),
                        pltpu.VMEM((8, TQ), jnp.int32), pltpu.VMEM((8, TQ), jnp.int32)],
        compiler_params=_params(("parallel", "arbitrary")),
        name="dsa_attention",
    )(q_idx, q_a, kvw)


DIL_PAIRS = 6


def _dilated_kernel(q_ref, k_ref, v_ref, bias_ref, o_ref, qm_scr, raw_scr, vt_scr, acc_scr, ot_scr):
    n = pl.program_id(2)
    nq = vt_scr.shape[1]
    nheads = 2 * DIL_PAIRS

    @pl.when(n == 0)
    def _():
        for p in range(DIL_PAIRS):
            for kb in range(nq):
                vt = v_ref[kb * TK:(kb + 1) * TK, p * LANES:(p + 1) * LANES].astype(F32).T
                vt_scr[2 * p, kb] = _ones_rows_from(vt, HEAD_DIM).astype(BF16)
                vt_scr[2 * p + 1, kb] = jnp.concatenate(
                    [vt[HEAD_DIM:], jnp.ones((LANES - HEAD_DIM, TK), F32)], axis=0).astype(BF16)

    lane = lax.broadcasted_iota(jnp.int32, (TQ, LANES), 1)
    for h in range(nheads):
        q = q_ref[:, (h // 2) * LANES:(h // 2 + 1) * LANES].astype(F32)
        qm_scr[h] = jnp.where((lane // HALF) % 2 == h % 2, q, 0.0).astype(BF16)

    def k_of(h, kb):
        return k_ref[pl.ds(pl.multiple_of(kb * TK, TK), TK), (h // 2) * LANES:(h // 2 + 1) * LANES]

    _attend_heads(nheads, lambda h: qm_scr[h], k_of, lambda kb: bias_ref[n - kb],
                  lambda h, kb: vt_scr[h, kb], raw_scr, acc_scr, ot_scr, n)
    _store_transposed(o_ref, ot_scr)


def _dilated(q_b, k_b, v_b, bias, *, batch, seq):
    nq = seq // TQ
    width = DIL_PAIRS * LANES
    nheads = 2 * DIL_PAIRS
    qblk = pl.BlockSpec((TQ, width), lambda b, g, n: (b * nq + n, g))
    kvblk = pl.BlockSpec((seq, width), lambda b, g, n: (b, g))
    return pl.pallas_call(
        _dilated_kernel,
        grid=(batch, B_PAIRS // DIL_PAIRS, nq),
        in_specs=[qblk, kvblk, kvblk, _resident(bias.shape)],
        out_specs=qblk,
        out_shape=jax.ShapeDtypeStruct(q_b.shape, BF16),
        scratch_shapes=[pltpu.VMEM((nheads, TQ, LANES), BF16), pltpu.VMEM((2, nheads, TK, TQ), F32),
                        pltpu.VMEM((nheads, nq, LANES, TK), BF16), pltpu.VMEM((nheads, LANES, TQ), F32),
                        pltpu.VMEM((nheads * HEAD_DIM, TQ), F32)],
        compiler_params=_params(("parallel", "parallel", "arbitrary")),
        name="dilated_attention",
    )(q_b, k_b, v_b, bias)


def _out_proj_kernel(x_ref, oa_ref, ob_ref, uc_ref, uprev_ref, wcol_ref, wpool_ref, pscale_ref,
                     wo_ref, o_ref, *, seq):
    tm = x_ref.shape[0]
    pos0 = (pl.program_id(0) * tm) % seq
    u = uc_ref[...].astype(F32)
    hist = jnp.where(pos0 == 0, 0.0, uprev_ref[...].astype(F32))
    ext = jnp.concatenate([hist, u], axis=0)
    s2 = ext + pltpu.roll(ext, 1, axis=0)
    s4 = s2 + pltpu.roll(s2, 2, axis=0)
    s8 = s4 + pltpu.roll(s4, 4, axis=0)
    s16 = s8 + pltpu.roll(s8, 8, axis=0)
    wcol = wcol_ref[...]
    sums = jnp.where(wcol == 2.0, s2, jnp.where(wcol == 4.0, s4, jnp.where(wcol == 8.0, s8, s16)))
    pos = (pos0 + lax.broadcasted_iota(jnp.int32, (tm, 1), 0)).astype(F32)
    y = sums[POOL_HIST:] / jnp.minimum(pos + 1.0, wcol) - u
    oc = (jnp.dot(y.astype(BF16), wpool_ref[...], preferred_element_type=F32) * pscale_ref[...]).astype(BF16)
    a_w, b_w = oa_ref.shape[1], ob_ref.shape[1]
    o_ref[...] = (x_ref[...]
                  + jnp.dot(oa_ref[...], wo_ref[:a_w], preferred_element_type=F32)
                  + jnp.dot(ob_ref[...], wo_ref[a_w:a_w + b_w], preferred_element_type=F32)
                  + jnp.dot(oc, wo_ref[a_w + b_w:], preferred_element_type=F32))


def _out_proj(x2, o_a, o_b, u_c, wcol, wpool, pscale, wo, *, layer, seq):
    n, d = x2.shape
    tm = 2 * TQ
    lres = functools.partial(_layer_resident, layer=layer)
    row = lambda a: pl.BlockSpec((tm, a.shape[1]), lambda i: (i, 0))
    hist = pl.BlockSpec((POOL_HIST, u_c.shape[1]),
                        lambda i: (jnp.maximum(i * (tm // POOL_HIST) - 1, 0), 0))
    return pl.pallas_call(
        functools.partial(_out_proj_kernel, seq=seq),
        grid=(n // tm,),
        in_specs=[row(x2), row(o_a), row(o_b), row(u_c), hist, _resident(wcol.shape),
                  lres(wpool), lres(pscale), lres(wo)],
        out_specs=row(x2),
        out_shape=jax.ShapeDtypeStruct((n, d), F32),
        compiler_params=_params(("parallel",)),
        name="out_proj",
    )(x2, o_a, o_b, u_c, u_c, wcol, wpool, pscale, wo)


def _mlp_kernel(x_ref, g_ref, wup_ref, wdn_ref, gfin_ref, o_ref, h_scr, *, final):
    j = pl.program_id(1)

    @pl.when(j == 0)
    def _():
        x = x_ref[...]
        h_scr[...] = (x * lax.rsqrt(jnp.mean(x * x, axis=-1, keepdims=True) + EPS) * g_ref[...]).astype(BF16)
        o_ref[...] = x

    a = jnp.dot(h_scr[...], wup_ref[...], preferred_element_type=F32)
    a = jnp.square(jnp.maximum(a, 0.0)).astype(BF16)
    o_ref[...] += jnp.dot(a, wdn_ref[...], preferred_element_type=F32)

    if final:
        @pl.when(j == pl.num_programs(1) - 1)
        def _():
            y = o_ref[...]
            o_ref[...] = y * lax.rsqrt(jnp.mean(y * y, axis=-1, keepdims=True) + EPS) * gfin_ref[...]


def _mlp(x2, g, wup, wdn, gfin, *, layer, tm, tf, final):
    n, d = x2.shape
    dff = wup.shape[-1]
    return pl.pallas_call(
        functools.partial(_mlp_kernel, final=final),
        grid=(n // tm, dff // tf),
        in_specs=[pl.BlockSpec((tm, d), lambda i, j: (i, 0)), _layer_resident(g, layer),
                  pl.BlockSpec((None, d, tf), lambda i, j: (layer, 0, j)),
                  pl.BlockSpec((None, tf, d), lambda i, j: (layer, j, 0)),
                  _resident(gfin.shape)],
        out_specs=pl.BlockSpec((tm, d), lambda i, j: (i, 0)),
        out_shape=jax.ShapeDtypeStruct((n, d), F32),
        scratch_shapes=[pltpu.VMEM((tm, d), BF16)],
        compiler_params=_params(("parallel", "arbitrary")),
        name="mlp",
    )(x2, g, wup, wdn, gfin)


def kernel(x, g_mix, w_in, g_cq, w_uq, w_uq_idx, w_pool, pool_scale, w_o, g_mlp, w_up, w_down, g_final):
    batch, seq, d_model = x.shape
    depth = w_in.shape[0]
    assert TQ == TK and seq % (2 * TQ) == 0
    cols, col_scale, a_q_rank, b_width, c_width = _in_layout(d_model)
    cos, sin_signed = _rope_tables(seq)
    dil_bias = _dilated_bias(seq // TK)
    groups, gdim = w_pool.shape[1], w_pool.shape[2]
    wcol = jnp.asarray(np.repeat(np.asarray(POOL_WINDOWS, np.float32), gdim)[None, :])
    qa_cols = np.concatenate([_padded_head_cols(h * HEAD_DIM) for h in range(A_HEADS)])
    qi_cols = np.concatenate([_padded_head_cols(h * HEAD_DIM) for h in range(IDX_HEADS)])

    w_in_p = (_take_cols(w_in, cols) * col_scale).astype(BF16)
    wq_p = jnp.concatenate([_take_cols(w_uq, qa_cols) * (HEAD_DIM ** -0.5),
                            _take_cols(w_uq_idx, qi_cols)], axis=-1).astype(BF16)
    wpool_p = jnp.stack([jax.scipy.linalg.block_diag(*[w_pool[l, g] for g in range(groups)])
                         for l in range(depth)]).astype(BF16)
    wo_p, wup_p, wdn_p = w_o.astype(BF16), w_up.astype(BF16), w_down.astype(BF16)
    g_mix_p, g_cq_p, g_mlp_p, pscale_p = (a[:, None, :] for a in (g_mix, g_cq, g_mlp, pool_scale))

    x2 = x.reshape(batch * seq, d_model)
    for l in range(depth):
        q_a, q_idx, kvw, q_b, k_b, v_b, u_c = _in_proj(
            x2, g_mix_p, w_in_p, g_cq_p, wq_p, cos, sin_signed, layer=l,
            batch=batch, seq=seq, a_q_rank=a_q_rank, b_width=b_width, c_width=c_width)
        o_a = _dsa(q_idx, q_a, kvw, batch=batch, seq=seq)
        o_b = _dilated(q_b, k_b, v_b, dil_bias, batch=batch, seq=seq)
        x2 = _out_proj(x2, o_a, o_b, u_c, wcol, wpool_p, pscale_p, wo_p, layer=l, seq=seq)
        x2 = _mlp(x2, g_mlp_p, wup_p, wdn_p, g_final[None, :], layer=l, tm=1024, tf=512,
                  final=(l == depth - 1))
    return x2.reshape(batch, seq, d_model)
```

```python
import functools

import numpy as np
import jax
import jax.numpy as jnp
from jax import lax
from jax.experimental import pallas as pl
from jax.experimental.pallas import tpu as pltpu

F32 = jnp.float32
BF16 = jnp.bfloat16

LANES = 128
VMEM_LIMIT = 56 * 1024 * 1024

HEAD_DIM = 64
HALF = HEAD_DIM // 2
ROPE_THETA = 10000.0
EPS = 1e-6
NEG = -(2.0 ** 100)
INT_MIN = -(2 ** 31)
A_HEADS = 8
IDX_HEADS = 16
TOPK_MAX = 256
B_HEADS = 12
B_PAIRS = B_HEADS // 2
DILATED_PATTERNS = ((128, 1), (512, 4), (2048, 16))
POOL_WINDOWS = (2, 4, 8, 16)
POOL_HIST = 16

TQ = 256
TK = 256
_NT = (((1,), (1,)), ((), ()))


def _params(sem):
    return pltpu.CompilerParams(dimension_semantics=sem, vmem_limit_bytes=VMEM_LIMIT)


def _resident(shape):
    return pl.BlockSpec(shape, lambda *_: (0,) * len(shape), pipeline_mode=pl.Buffered(1))


def _layer_resident(stacked, layer):
    rest = stacked.shape[1:]
    return pl.BlockSpec((None, *rest), lambda *_: (layer,) + (0,) * len(rest),
                        pipeline_mode=pl.Buffered(1))


def _padded_head_cols(base):
    cols = np.full((LANES,), -1, np.int64)
    cols[0:HALF] = base + np.arange(HALF)
    cols[2 * HALF:3 * HALF] = base + HALF + np.arange(HALF)
    return cols


def _pair_head_cols(base):
    h0, h1 = base, base + HEAD_DIM
    return np.concatenate([h0 + np.arange(HALF), h1 + np.arange(HALF),
                           h0 + HALF + np.arange(HALF), h1 + HALF + np.arange(HALF)])


def _take_cols(w, cols):
    pieces, start = [], 0
    for end in range(1, len(cols) + 1):
        run_ends = end == len(cols) or (cols[end] != cols[end - 1] + 1 if cols[end - 1] >= 0
                                        else cols[end] >= 0)
        if run_ends:
            c0 = int(cols[start])
            pieces.append(jnp.zeros((*w.shape[:-1], end - start), w.dtype) if c0 < 0
                          else w[..., c0:c0 + end - start])
            start = end
    return jnp.concatenate(pieces, axis=-1)


def _in_layout(d_model):
    a_q_rank = 3 * d_model // 16
    b_width = B_HEADS * HEAD_DIM
    c_width = 3 * d_model // 8
    sizes = (a_q_rank, HEAD_DIM, HEAD_DIM, HEAD_DIM, IDX_HEADS, b_width, b_width, b_width, c_width)
    off = np.concatenate([[0], np.cumsum(sizes)])
    o_cq, o_ka, o_va, o_ki, o_wi, o_qb, o_kb, o_vb, o_uc = (int(v) for v in off[:-1])
    pad = np.full((LANES,), -1, np.int64)
    va = pad.copy()
    va[:HEAD_DIM] = o_va + np.arange(HEAD_DIM)
    wi = pad.copy()
    wi[:IDX_HEADS] = o_wi + np.arange(IDX_HEADS)
    cols = np.concatenate(
        [o_cq + np.arange(a_q_rank),
         _padded_head_cols(o_ka), _padded_head_cols(o_ki), va, wi]
        + [_pair_head_cols(o_qb + p * LANES) for p in range(B_PAIRS)]
        + [_pair_head_cols(o_kb + p * LANES) for p in range(B_PAIRS)]
        + [o_vb + np.arange(b_width), o_uc + np.arange(c_width)])
    scale = np.ones((cols.shape[0],), np.float32)
    wi_start = a_q_rank + 3 * LANES
    scale[wi_start:wi_start + LANES] = (IDX_HEADS ** -0.5) * (HEAD_DIM ** -0.5)
    qb_start = a_q_rank + 4 * LANES
    scale[qb_start:qb_start + b_width] = HEAD_DIM ** -0.5
    return cols, scale, a_q_rank, b_width, c_width


def _rope_tables(seq):
    inv = ROPE_THETA ** (-jnp.arange(HALF, dtype=F32) / HALF)
    ang = jnp.arange(seq, dtype=F32)[:, None] * inv[None, :]
    cos, sin = jnp.cos(ang), jnp.sin(ang)
    return (jnp.concatenate([cos, cos, cos, cos], axis=1),
            jnp.concatenate([-sin, -sin, sin, sin], axis=1))


def _dilated_bias(nkb):
    d = np.arange(nkb)[:, None, None]
    delta = d * TK + np.arange(TQ)[None, None, :] - np.arange(TK)[None, :, None]
    count = np.zeros(delta.shape, np.float64)
    for window, dil in DILATED_PATTERNS:
        count += (delta >= 0) & (delta % dil == 0) & (delta <= window)
    with np.errstate(divide="ignore"):
        bias = np.where(count > 0, np.log(np.maximum(count, 1.0)), NEG)
    return jnp.asarray(bias, F32)


def _rope_store(dst_ref, x, cos, sin_signed):
    for j in range(x.shape[1] // LANES):
        xb = x[:, j * LANES:(j + 1) * LANES]
        y = (xb * cos + pltpu.roll(xb, 2 * HALF, axis=1) * sin_signed).astype(dst_ref.dtype)
        if len(dst_ref.shape) == 3:
            dst_ref[j] = y
        else:
            dst_ref[:, j * LANES:(j + 1) * LANES] = y


def _in_proj_kernel(x_ref, g_ref, w_ref, gcq_ref, wq_ref, cos_ref, sin_ref,
                    qa_ref, qi_ref, kvw_ref, qb_ref, kb_ref, vb_ref, uc_ref,
                    *, a_q_rank, b_width):
    x = x_ref[...]
    h = (x * lax.rsqrt(jnp.mean(x * x, axis=-1, keepdims=True) + EPS) * g_ref[...]).astype(BF16)
    cos, sin_signed = cos_ref[...], sin_ref[...]

    def proj(start, width):
        return jnp.dot(h, w_ref[:, start:start + width], preferred_element_type=F32)

    cq = proj(0, a_q_rank)
    cqn = (cq * lax.rsqrt(jnp.mean(cq * cq, axis=-1, keepdims=True) + EPS) * gcq_ref[...]).astype(BF16)
    qa_w = A_HEADS * LANES
    _rope_store(qa_ref, jnp.dot(cqn, wq_ref[:, :qa_w], preferred_element_type=F32), cos, sin_signed)
    _rope_store(qi_ref, jnp.dot(cqn, wq_ref[:, qa_w:], preferred_element_type=F32), cos, sin_signed)

    kvw = proj(a_q_rank, 4 * LANES)
    _rope_store(kvw_ref.at[:, :2 * LANES], kvw[:, :2 * LANES], cos, sin_signed)
    kvw_ref[:, 2 * LANES:] = kvw[:, 2 * LANES:].astype(BF16)
    start = a_q_rank + 4 * LANES
    _rope_store(qb_ref, proj(start, b_width), cos, sin_signed)
    _rope_store(kb_ref, proj(start + b_width, b_width), cos, sin_signed)
    vb_ref[...] = proj(start + 2 * b_width, b_width).astype(BF16)
    uc_ref[...] = proj(start + 3 * b_width, uc_ref.shape[1]).astype(BF16)


def _in_proj(x2, g, w, gcq, wq, cos, sin_signed, *, layer, batch, seq, a_q_rank, b_width, c_width):
    n, d = x2.shape
    lres = functools.partial(_layer_resident, layer=layer)
    tm = 2 * TQ
    nq = seq // tm
    row = lambda width: pl.BlockSpec((tm, width), lambda i: (i, 0))
    tab = pl.BlockSpec((tm, LANES), lambda i: (i % nq, 0))
    heads = lambda nh: pl.BlockSpec((None, nh, tm, LANES), lambda i: (i // nq, 0, i % nq, 0))
    qshape = lambda nh: jax.ShapeDtypeStruct((batch, nh, seq, LANES), BF16)
    widths = (4 * LANES, b_width, b_width, b_width, c_width)
    return pl.pallas_call(
        functools.partial(_in_proj_kernel, a_q_rank=a_q_rank, b_width=b_width),
        grid=(n // tm,),
        in_specs=[row(d), lres(g), lres(w), lres(gcq), lres(wq), tab, tab],
        out_specs=[heads(A_HEADS), heads(IDX_HEADS)] + [row(wd) for wd in widths],
        out_shape=[qshape(A_HEADS), qshape(IDX_HEADS)]
                  + [jax.ShapeDtypeStruct((n, wd), BF16) for wd in widths],
        compiler_params=_params(("parallel",)),
        name="in_proj",
    )(x2, g, w, gcq, wq, cos, sin_signed)


def _attend_heads(nheads, q_of, k_of, bias_block, vt_of, raw_scr, acc_scr, ot_scr, last):
    def qk(h, kb, slot):
        raw_scr[slot, h] = lax.dot_general(k_of(h, kb), q_of(h), _NT, preferred_element_type=F32)

    def softmax_pv(h, kb, slot, m_old, bias):
        raw = raw_scr[slot, h]
        lg = raw.astype(BF16) + bias if bias.dtype == BF16 else (raw + bias).astype(BF16)
        m_new = jnp.maximum(m_old, jnp.max(lg, axis=0, keepdims=True).astype(F32))
        p = jnp.exp(lg - m_new.astype(BF16))
        acc_scr[h] = (acc_scr[h] * jnp.exp(m_old - m_new)
                      + jnp.dot(vt_of(h, kb), p, preferred_element_type=F32))
        return m_new

    def block(kb, slot, ms, ahead):
        bias = bias_block(kb)
        new = []
        for h in range(nheads):
            if ahead is not None:
                qk(h, ahead, 1 - slot)
            new.append(softmax_pv(h, kb, slot, ms[h], bias))
        return tuple(new)

    def two_blocks(i, ms):
        kb = 2 * i
        return block(kb + 1, 1, block(kb, 0, ms, kb + 1), kb + 2)

    acc_scr[...] = jnp.zeros(acc_scr.shape, F32)
    for h in range(nheads):
        qk(h, 0, 0)
    pairs = last // 2
    ms = lax.fori_loop(0, pairs, two_blocks, (jnp.full((1, TQ), NEG, F32),) * nheads)

    @pl.when(last % 2 == 0)
    def _():
        block(last, 0, ms, None)

    @pl.when(last % 2 == 1)
    def _():
        block(last, 1, block(last - 1, 0, ms, last), None)

    for h in range(nheads):
        acc = acc_scr[h]
        ot_scr[h * HEAD_DIM:(h + 1) * HEAD_DIM, :] = acc[:HEAD_DIM] * (1.0 / acc[HEAD_DIM:HEAD_DIM + 1])


def _ones_rows_from(vt, first_row):
    row = lax.broadcasted_iota(jnp.int32, vt.shape, 0)
    return jnp.where(row >= first_row, 1.0, vt)


def _store_transposed(o_ref, ot_scr):
    for c in range(ot_scr.shape[0] // LANES):
        o_ref[:, c * LANES:(c + 1) * LANES] = ot_scr[c * LANES:(c + 1) * LANES, :].T.astype(o_ref.dtype)


def _select_threshold(key_scr, hi_scr, lo_scr, ntiles, topk):
    i16_min, i16_max = -(2 ** 15), 2 ** 15 - 1
    slab = 16
    chains = 4

    def count_ge(scr, cand):
        c16 = cand.astype(jnp.int16)
        parts = [jnp.zeros((slab, TQ), jnp.int16)] * chains
        for i in range(ntiles):
            m = jnp.where(scr[i] >= c16, jnp.int16(1), jnp.int16(0))
            for r in range(TK // slab):
                parts[r % chains] = parts[r % chains] + m[r * slab:(r + 1) * slab]
        part = functools.reduce(lambda a, b: a + b, parts)
        return jnp.sum(part.astype(jnp.int32), axis=0, keepdims=True)

    def radix16(scr, need, n_all):
        def refine(state, cand):
            thr, n_ge = state
            c = count_ge(scr, cand)
            ok = c >= need
            return jnp.where(ok, cand, thr), jnp.where(ok, c, n_ge)

        state = refine((jnp.full((1, TQ), i16_min, jnp.int32), n_all), jnp.zeros((1, TQ), jnp.int32))
        return lax.fori_loop(
            0, 15, lambda b, st: refine(st, st[0] | jnp.left_shift(jnp.int32(1), 14 - b)), state)

    for i in range(ntiles):
        hi_scr[i] = (key_scr[i] >> 16).astype(jnp.int16)
    n_all = jnp.full((1, TQ), ntiles * TK, jnp.int32)
    t_hi, n_hi = radix16(hi_scr, topk, n_all)
    n_above = jnp.where(t_hi == i16_max, 0, count_ge(hi_scr, jnp.minimum(t_hi + 1, i16_max)))

    t_hi16 = t_hi.astype(jnp.int16)
    for i in range(ntiles):
        lo = ((key_scr[i] & 0xFFFF) + i16_min).astype(jnp.int16)
        lo_scr[i] = jnp.where(hi_scr[i] == t_hi16, lo, jnp.int16(i16_min))
    t_lo, n_lo = radix16(lo_scr, topk - n_above, n_hi - n_above)
    return jnp.left_shift(t_hi, 16) | (t_lo - i16_min), n_above + n_lo


def _dsa_kernel(qi_ref, qa_ref, kvw_ref, o_ref, key_scr, hi_scr, lo_scr, bias_scr, raw_scr, vt_scr,
                acc_scr, ot_scr, thr_scr, cnt_scr, *, topk):
    n = pl.program_id(1)
    nq = key_scr.shape[0]

    @pl.when(n == 0)
    def _():
        for kb in range(nq):
            vt = kvw_ref[kb * TK:(kb + 1) * TK, 2 * LANES:3 * LANES].astype(F32).T
            vt_scr[kb] = _ones_rows_from(vt, HEAD_DIM).astype(BF16)

    def rows(kb, c0):
        return kvw_ref[pl.ds(pl.multiple_of(kb * TK, TK), TK), c0:c0 + LANES]

    w_t = kvw_ref[pl.ds(pl.multiple_of(n * TQ, TQ), TQ), 3 * LANES:4 * LANES].astype(F32).T
    s_loc = lax.broadcasted_iota(jnp.int32, (TK, TQ), 0)
    t_loc = lax.broadcasted_iota(jnp.int32, (TK, TQ), 1)

    def score_block(kb):
        k_idx = rows(kb, LANES)
        acc = jnp.zeros((TK, TQ), F32)
        for h in range(IDX_HEADS):
            d = lax.dot_general(k_idx, qi_ref[h], _NT, preferred_element_type=F32)
            acc = acc + jnp.maximum(d, 0.0) * w_t[h:h + 1, :]
        bits = pltpu.bitcast(acc, jnp.int32)
        key = bits ^ ((bits >> 31) & jnp.int32(0x7FFFFFFF))
        admissible = (kb - n) * TK + s_loc <= t_loc
        key_scr[kb] = jnp.where(admissible, key, jnp.int32(INT_MIN))

    def four_score_blocks(i, carry):
        for r in range(4):
            score_block(4 * i + r)
        return carry

    quads = (n + 1) // 4
    lax.fori_loop(0, quads, four_score_blocks, 0)
    left = (n + 1) % 4

    @pl.when(left >= 2)
    def _():
        score_block(4 * quads)
        score_block(4 * quads + 1)

    @pl.when(left % 2 == 1)
    def _():
        score_block(n)

    for c in range(nq):
        @pl.when(n == c)
        def _(c=c):
            thr, n_ge = _select_threshold(key_scr, hi_scr, lo_scr, c + 1, topk)
            thr_scr[...] = jnp.broadcast_to(thr, thr_scr.shape)
            cnt_scr[...] = jnp.broadcast_to(n_ge, cnt_scr.shape)

    thr = thr_scr[0:1, :]
    n_ge = cnt_scr[0:1, :]
    floor = jnp.maximum(thr, jnp.int32(INT_MIN + 1))

    def bias_block(kb, carry):
        bias_scr[kb] = jnp.where(key_scr[kb] >= floor, 0.0, NEG).astype(BF16)
        return carry

    lax.fori_loop(0, n + 1, bias_block, 0)

    tied = jnp.where(thr != INT_MIN, jnp.where(n_ge > topk, 1.0, 0.0), 0.0)

    @pl.when(jnp.max(tied) > 0.0)
    def _():
        before = (lax.broadcasted_iota(jnp.int32, (TK, TK), 0)
                  > lax.broadcasted_iota(jnp.int32, (TK, TK), 1)).astype(BF16)

        def count_eq(kb, c):
            return c + jnp.sum(jnp.where(key_scr[kb] == thr, 1.0, 0.0), axis=0, keepdims=True)

        n_eq = lax.fori_loop(0, n + 1, count_eq, jnp.zeros((1, TQ), F32))
        need = topk - (n_ge.astype(F32) - n_eq)

        def rebias(kb, seen):
            key = key_scr[kb]
            eq = jnp.where(key == thr, 1.0, 0.0)
            rank = jnp.dot(before, eq.astype(BF16), preferred_element_type=F32) + seen
            keep = jnp.where(key > thr, 1.0, jnp.where(rank < need, eq, 0.0))
            bias_scr[kb] = jnp.where(tied > 0.0, jnp.where(keep > 0.0, 0.0, NEG),
                                     bias_scr[kb].astype(F32)).astype(BF16)
            return seen + jnp.sum(eq, axis=0, keepdims=True)

        lax.fori_loop(0, n + 1, rebias, jnp.zeros((1, TQ), F32))

    _attend_heads(A_HEADS, lambda h: qa_ref[h], lambda h, kb: rows(kb, 0), lambda kb: bias_scr[kb],
                  lambda h, kb: vt_scr[kb], raw_scr, acc_scr, ot_scr, n)
    _store_transposed(o_ref, ot_scr)


def _dsa(q_idx, q_a, kvw, *, batch, seq):
    nq = seq // TQ
    topk = min(TOPK_MAX, seq // 4)
    qblk = lambda nh: pl.BlockSpec((None, nh, TQ, LANES), lambda b, n: (b, 0, n, 0))
    width = A_HEADS * HEAD_DIM
    tile = lambda dtype: pltpu.VMEM((nq, TK, TQ), dtype)
    return pl.pallas_call(
        functools.partial(_dsa_kernel, topk=topk),
        grid=(batch, nq),
        in_specs=[qblk(IDX_HEADS), qblk(A_HEADS), pl.BlockSpec((seq, 4 * LANES), lambda b, n: (b, 0))],
        out_specs=pl.BlockSpec((TQ, width), lambda b, n: (b * nq + n, 0)),
        out_shape=jax.ShapeDtypeStruct((batch * seq, width), BF16),
        scratch_shapes=[tile(jnp.int32), tile(jnp.int16), tile(jnp.int16), tile(BF16),
                        pltpu.VMEM((2, A_HEADS, TK, TQ), F32), pltpu.VMEM((nq, LANES, TK), BF16),
                        pltpu.VMEM((A_HEADS, LANES, TQ), F32), pltpu.VMEM((width, TQ), F32),
                        pltpu.VMEM((8, TQ), jnp.int32), pltpu.VMEM((8, TQ), jnp.int32)],
        compiler_params=_params(("parallel", "arbitrary")),
        name="dsa_attention",
    )(q_idx, q_a, kvw)


DIL_PAIRS = 6


def _dilated_kernel(q_ref, k_ref, v_ref, bias_ref, o_ref, qm_scr, raw_scr, vt_scr, acc_scr, ot_scr):
    n = pl.program_id(2)
    nq = vt_scr.shape[1]
    nheads = 2 * DIL_PAIRS

    @pl.when(n == 0)
    def _():
        for p in range(DIL_PAIRS):
            for kb in range(nq):
                vt = v_ref[kb * TK:(kb + 1) * TK, p * LANES:(p + 1) * LANES].astype(F32).T
                vt_scr[2 * p, kb] = _ones_rows_from(vt, HEAD_DIM).astype(BF16)
                vt_scr[2 * p + 1, kb] = jnp.concatenate(
                    [vt[HEAD_DIM:], jnp.ones((LANES - HEAD_DIM, TK), F32)], axis=0).astype(BF16)

    lane = lax.broadcasted_iota(jnp.int32, (TQ, LANES), 1)
    for h in range(nheads):
        q = q_ref[:, (h // 2) * LANES:(h // 2 + 1) * LANES].astype(F32)
        qm_scr[h] = jnp.where((lane // HALF) % 2 == h % 2, q, 0.0).astype(BF16)

    def k_of(h, kb):
        return k_ref[pl.ds(pl.multiple_of(kb * TK, TK), TK), (h // 2) * LANES:(h // 2 + 1) * LANES]

    _attend_heads(nheads, lambda h: qm_scr[h], k_of, lambda kb: bias_ref[n - kb],
                  lambda h, kb: vt_scr[h, kb], raw_scr, acc_scr, ot_scr, n)
    _store_transposed(o_ref, ot_scr)


def _dilated(q_b, k_b, v_b, bias, *, batch, seq):
    nq = seq // TQ
    width = DIL_PAIRS * LANES
    nheads = 2 * DIL_PAIRS
    qblk = pl.BlockSpec((TQ, width), lambda b, g, n: (b * nq + n, g))
    kvblk = pl.BlockSpec((seq, width), lambda b, g, n: (b, g))
    return pl.pallas_call(
        _dilated_kernel,
        grid=(batch, B_PAIRS // DIL_PAIRS, nq),
        in_specs=[qblk, kvblk, kvblk, _resident(bias.shape)],
        out_specs=qblk,
        out_shape=jax.ShapeDtypeStruct(q_b.shape, BF16),
        scratch_shapes=[pltpu.VMEM((nheads, TQ, LANES), BF16), pltpu.VMEM((2, nheads, TK, TQ), F32),
                        pltpu.VMEM((nheads, nq, LANES, TK), BF16), pltpu.VMEM((nheads, LANES, TQ), F32),
                        pltpu.VMEM((nheads * HEAD_DIM, TQ), F32)],
        compiler_params=_params(("parallel", "parallel", "arbitrary")),
        name="dilated_attention",
    )(q_b, k_b, v_b, bias)


def _out_proj_kernel(x_ref, oa_ref, ob_ref, uc_ref, uprev_ref, wcol_ref, wpool_ref, pscale_ref,
                     wo_ref, o_ref, *, seq):
    tm = x_ref.shape[0]
    pos0 = (pl.program_id(0) * tm) % seq
    u = uc_ref[...].astype(F32)
    hist = jnp.where(pos0 == 0, 0.0, uprev_ref[...].astype(F32))
    ext = jnp.concatenate([hist, u], axis=0)
    s2 = ext + pltpu.roll(ext, 1, axis=0)
    s4 = s2 + pltpu.roll(s2, 2, axis=0)
    s8 = s4 + pltpu.roll(s4, 4, axis=0)
    s16 = s8 + pltpu.roll(s8, 8, axis=0)
    wcol = wcol_ref[...]
    sums = jnp.where(wcol == 2.0, s2, jnp.where(wcol == 4.0, s4, jnp.where(wcol == 8.0, s8, s16)))
    pos = (pos0 + lax.broadcasted_iota(jnp.int32, (tm, 1), 0)).astype(F32)
    y = sums[POOL_HIST:] / jnp.minimum(pos + 1.0, wcol) - u
    oc = (jnp.dot(y.astype(BF16), wpool_ref[...], preferred_element_type=F32) * pscale_ref[...]).astype(BF16)
    a_w, b_w = oa_ref.shape[1], ob_ref.shape[1]
    o_ref[...] = (x_ref[...]
                  + jnp.dot(oa_ref[...], wo_ref[:a_w], preferred_element_type=F32)
                  + jnp.dot(ob_ref[...], wo_ref[a_w:a_w + b_w], preferred_element_type=F32)
                  + jnp.dot(oc, wo_ref[a_w + b_w:], preferred_element_type=F32))


def _out_proj(x2, o_a, o_b, u_c, wcol, wpool, pscale, wo, *, layer, seq):
    n, d = x2.shape
    tm = 2 * TQ
    lres = functools.partial(_layer_resident, layer=layer)
    row = lambda a: pl.BlockSpec((tm, a.shape[1]), lambda i: (i, 0))
    hist = pl.BlockSpec((POOL_HIST, u_c.shape[1]),
                        lambda i: (jnp.maximum(i * (tm // POOL_HIST) - 1, 0), 0))
    return pl.pallas_call(
        functools.partial(_out_proj_kernel, seq=seq),
        grid=(n // tm,),
        in_specs=[row(x2), row(o_a), row(o_b), row(u_c), hist, _resident(wcol.shape),
                  lres(wpool), lres(pscale), lres(wo)],
        out_specs=row(x2),
        out_shape=jax.ShapeDtypeStruct((n, d), F32),
        compiler_params=_params(("parallel",)),
        name="out_proj",
    )(x2, o_a, o_b, u_c, u_c, wcol, wpool, pscale, wo)


def _mlp_kernel(x_ref, g_ref, wup_ref, wdn_ref, gfin_ref, o_ref, h_scr, *, final):
    j = pl.program_id(1)

    @pl.when(j == 0)
    def _():
        x = x_ref[...]
        h_scr[...] = (x * lax.rsqrt(jnp.mean(x * x, axis=-1, keepdims=True) + EPS) * g_ref[...]).astype(BF16)
        o_ref[...] = x

    a = jnp.dot(h_scr[...], wup_ref[...], preferred_element_type=F32)
    a = jnp.square(jnp.maximum(a, 0.0)).astype(BF16)
    o_ref[...] += jnp.dot(a, wdn_ref[...], preferred_element_type=F32)

    if final:
        @pl.when(j == pl.num_programs(1) - 1)
        def _():
            y = o_ref[...]
            o_ref[...] = y * lax.rsqrt(jnp.mean(y * y, axis=-1, keepdims=True) + EPS) * gfin_ref[...]


def _mlp(x2, g, wup, wdn, gfin, *, layer, tm, tf, final):
    n, d = x2.shape
    dff = wup.shape[-1]
    return pl.pallas_call(
        functools.partial(_mlp_kernel, final=final),
        grid=(n // tm, dff // tf),
        in_specs=[pl.BlockSpec((tm, d), lambda i, j: (i, 0)), _layer_resident(g, layer),
                  pl.BlockSpec((None, d, tf), lambda i, j: (layer, 0, j)),
                  pl.BlockSpec((None, tf, d), lambda i, j: (layer, j, 0)),
                  _resident(gfin.shape)],
        out_specs=pl.BlockSpec((tm, d), lambda i, j: (i, 0)),
        out_shape=jax.ShapeDtypeStruct((n, d), F32),
        scratch_shapes=[pltpu.VMEM((tm, d), BF16)],
        compiler_params=_params(("parallel", "arbitrary")),
        name="mlp",
    )(x2, g, wup, wdn, gfin)


def kernel(x, g_mix, w_in, g_cq, w_uq, w_uq_idx, w_pool, pool_scale, w_o, g_mlp, w_up, w_down, g_final):
    batch, seq, d_model = x.shape
    depth = w_in.shape[0]
    assert TQ == TK and seq % (2 * TQ) == 0
    cols, col_scale, a_q_rank, b_width, c_width = _in_layout(d_model)
    cos, sin_signed = _rope_tables(seq)
    dil_bias = _dilated_bias(seq // TK)
    groups, gdim = w_pool.shape[1], w_pool.shape[2]
    wcol = jnp.asarray(np.repeat(np.asarray(POOL_WINDOWS, np.float32), gdim)[None, :])
    qa_cols = np.concatenate([_padded_head_cols(h * HEAD_DIM) for h in range(A_HEADS)])
    qi_cols = np.concatenate([_padded_head_cols(h * HEAD_DIM) for h in range(IDX_HEADS)])

    w_in_p = (_take_cols(w_in, cols) * col_scale).astype(BF16)
    wq_p = jnp.concatenate([_take_cols(w_uq, qa_cols) * (HEAD_DIM ** -0.5),
                            _take_cols(w_uq_idx, qi_cols)], axis=-1).astype(BF16)
    wpool_p = jnp.stack([jax.scipy.linalg.block_diag(*[w_pool[l, g] for g in range(groups)])
                         for l in range(depth)]).astype(BF16)
    wo_p, wup_p, wdn_p = w_o.astype(BF16), w_up.astype(BF16), w_down.astype(BF16)
    g_mix_p, g_cq_p, g_mlp_p, pscale_p = (a[:, None, :] for a in (g_mix, g_cq, g_mlp, pool_scale))

    x2 = x.reshape(batch * seq, d_model)
    for l in range(depth):
        q_a, q_idx, kvw, q_b, k_b, v_b, u_c = _in_proj(
            x2, g_mix_p, w_in_p, g_cq_p, wq_p, cos, sin_signed, layer=l,
            batch=batch, seq=seq, a_q_rank=a_q_rank, b_width=b_width, c_width=c_width)
        o_a = _dsa(q_idx, q_a, kvw, batch=batch, seq=seq)
        o_b = _dilated(q_b, k_b, v_b, dil_bias, batch=batch, seq=seq)
        x2 = _out_proj(x2, o_a, o_b, u_c, wcol, wpool_p, pscale_p, wo_p, layer=l, seq=seq)
        x2 = _mlp(x2, g_mlp_p, wup_p, wdn_p, g_final[None, :], layer=l, tm=1024, tf=512,
                  final=(l == depth - 1))
    return x2.reshape(batch, seq, d_model)
```

```python
import functools

import numpy as np
import jax
import jax.numpy as jnp
from jax import lax
from jax.experimental import pallas as pl
from jax.experimental.pallas import tpu as pltpu

F32 = jnp.float32
BF16 = jnp.bfloat16

LANES = 128
VMEM_LIMIT = 56 * 1024 * 1024

HEAD_DIM = 64
HALF = HEAD_DIM // 2
ROPE_THETA = 10000.0
EPS = 1e-6
NEG = -(2.0 ** 100)
INT_MIN = -(2 ** 31)
A_HEADS = 8
IDX_HEADS = 16
TOPK_MAX = 256
B_HEADS = 12
B_PAIRS = B_HEADS // 2
DILATED_PATTERNS = ((128, 1), (512, 4), (2048, 16))
POOL_WINDOWS = (2, 4, 8, 16)
POOL_HIST = 16

TQ = 256
TK = 256
_NT = (((1,), (1,)), ((), ()))


def _params(sem):
    return pltpu.CompilerParams(dimension_semantics=sem, vmem_limit_bytes=VMEM_LIMIT)


def _resident(shape):
    return pl.BlockSpec(shape, lambda *_: (0,) * len(shape), pipeline_mode=pl.Buffered(1))


def _layer_resident(stacked, layer):
    rest = stacked.shape[1:]
    return pl.BlockSpec((None, *rest), lambda *_: (layer,) + (0,) * len(rest),
                        pipeline_mode=pl.Buffered(1))


def _padded_heads(w, nheads):
    lead = w.shape[:-1]
    x = jnp.pad(w.reshape(*lead, nheads, 2, HALF), [(0, 0)] * (len(lead) + 2) + [(0, HALF)])
    return x.reshape(*lead, nheads * LANES)


def _paired_heads(w, npairs):
    lead = w.shape[:-1]
    x = jnp.swapaxes(w.reshape(*lead, npairs, 2, 2, HALF), -3, -2)
    return x.reshape(*lead, npairs * LANES)


def _zero_pad(w, width):
    return jnp.pad(w, [(0, 0)] * (w.ndim - 1) + [(0, width - w.shape[-1])])


def _in_layout(w_in, d_model):
    a_q_rank = 3 * d_model // 16
    b_width = B_HEADS * HEAD_DIM
    c_width = 3 * d_model // 8
    sizes = (a_q_rank, HEAD_DIM, HEAD_DIM, HEAD_DIM, IDX_HEADS, b_width, b_width, b_width, c_width)
    cq, ka, va, ki, wi, qb, kb, vb, uc = jnp.split(w_in, np.cumsum(sizes)[:-1].tolist(), axis=-1)
    w = jnp.concatenate(
        [cq, _padded_heads(ka, 1), _padded_heads(ki, 1), _zero_pad(va, LANES),
         _zero_pad(wi * ((IDX_HEADS ** -0.5) * (HEAD_DIM ** -0.5)), LANES),
         _paired_heads(qb, B_PAIRS) * (HEAD_DIM ** -0.5), _paired_heads(kb, B_PAIRS), vb, uc], axis=-1)
    return w.astype(BF16), a_q_rank, b_width, c_width


def _rope_tables(seq):
    inv = ROPE_THETA ** (-jnp.arange(HALF, dtype=F32) / HALF)
    ang = jnp.arange(seq, dtype=F32)[:, None] * inv[None, :]
    cos, sin = jnp.cos(ang), jnp.sin(ang)
    return (jnp.concatenate([cos, cos, cos, cos], axis=1),
            jnp.concatenate([-sin, -sin, sin, sin], axis=1))


def _dilated_bias(nkb):
    d = np.arange(nkb)[:, None, None]
    delta = d * TK + np.arange(TQ)[None, None, :] - np.arange(TK)[None, :, None]
    count = np.zeros(delta.shape, np.float64)
    for window, dil in DILATED_PATTERNS:
        count += (delta >= 0) & (delta % dil == 0) & (delta <= window)
    with np.errstate(divide="ignore"):
        bias = np.where(count > 0, np.log(np.maximum(count, 1.0)), NEG)
    return jnp.asarray(bias, F32)


def _rope_store(dst_ref, x, cos, sin_signed):
    for j in range(x.shape[1] // LANES):
        xb = x[:, j * LANES:(j + 1) * LANES]
        y = (xb * cos + pltpu.roll(xb, 2 * HALF, axis=1) * sin_signed).astype(dst_ref.dtype)
        if len(dst_ref.shape) == 3:
            dst_ref[j] = y
        else:
            dst_ref[:, j * LANES:(j + 1) * LANES] = y


def _in_proj_kernel(x_ref, g_ref, w_ref, gcq_ref, wq_ref, cos_ref, sin_ref,
                    qa_ref, qi_ref, kvw_ref, qb_ref, kb_ref, vb_ref, uc_ref,
                    *, a_q_rank, b_width):
    x = x_ref[...]
    h = (x * lax.rsqrt(jnp.mean(x * x, axis=-1, keepdims=True) + EPS) * g_ref[...]).astype(BF16)
    cos, sin_signed = cos_ref[...], sin_ref[...]

    def proj(start, width):
        return jnp.dot(h, w_ref[:, start:start + width], preferred_element_type=F32)

    cq = proj(0, a_q_rank)
    cqn = (cq * lax.rsqrt(jnp.mean(cq * cq, axis=-1, keepdims=True) + EPS) * gcq_ref[...]).astype(BF16)
    qa_w = A_HEADS * LANES
    _rope_store(qa_ref, jnp.dot(cqn, wq_ref[:, :qa_w], preferred_element_type=F32), cos, sin_signed)
    _rope_store(qi_ref, jnp.dot(cqn, wq_ref[:, qa_w:], preferred_element_type=F32), cos, sin_signed)

    kvw = proj(a_q_rank, 4 * LANES)
    _rope_store(kvw_ref.at[:, :2 * LANES], kvw[:, :2 * LANES], cos, sin_signed)
    kvw_ref[:, 2 * LANES:] = kvw[:, 2 * LANES:].astype(BF16)
    start = a_q_rank + 4 * LANES
    _rope_store(qb_ref, proj(start, b_width), cos, sin_signed)
    _rope_store(kb_ref, proj(start + b_width, b_width), cos, sin_signed)
    vb_ref[...] = proj(start + 2 * b_width, b_width).astype(BF16)
    uc_ref[...] = proj(start + 3 * b_width, uc_ref.shape[1]).astype(BF16)


def _in_proj(x2, g, w, gcq, wq, cos, sin_signed, *, layer, batch, seq, a_q_rank, b_width, c_width):
    n, d = x2.shape
    lres = functools.partial(_layer_resident, layer=layer)
    tm = 2 * TQ
    nq = seq // tm
    row = lambda width: pl.BlockSpec((tm, width), lambda i: (i, 0))
    tab = pl.BlockSpec((tm, LANES), lambda i: (i % nq, 0))
    heads = lambda nh: pl.BlockSpec((None, nh, tm, LANES), lambda i: (i // nq, 0, i % nq, 0))
    qshape = lambda nh: jax.ShapeDtypeStruct((batch, nh, seq, LANES), BF16)
    widths = (4 * LANES, b_width, b_width, b_width, c_width)
    return pl.pallas_call(
        functools.partial(_in_proj_kernel, a_q_rank=a_q_rank, b_width=b_width),
        grid=(n // tm,),
        in_specs=[row(d), lres(g), lres(w), lres(gcq), lres(wq), tab, tab],
        out_specs=[heads(A_HEADS), heads(IDX_HEADS)] + [row(wd) for wd in widths],
        out_shape=[qshape(A_HEADS), qshape(IDX_HEADS)]
                  + [jax.ShapeDtypeStruct((n, wd), BF16) for wd in widths],
        compiler_params=_params(("parallel",)),
        name="in_proj",
    )(x2, g, w, gcq, wq, cos, sin_signed)


def _attend_heads(nheads, q_of, k_of, bias_block, vt_of, raw_scr, acc_scr, ot_scr, last):
    def qk(h, kb, slot):
        raw_scr[slot, h] = lax.dot_general(k_of(h, kb), q_of(h), _NT, preferred_element_type=F32)

    def softmax_pv(h, kb, slot, m_old, bias):
        raw = raw_scr[slot, h]
        lg = raw.astype(BF16) + bias if bias.dtype == BF16 else (raw + bias).astype(BF16)
        m_new = jnp.maximum(m_old, jnp.max(lg, axis=0, keepdims=True).astype(F32))
        p = jnp.exp(lg - m_new.astype(BF16))
        acc_scr[h] = (acc_scr[h] * jnp.exp(m_old - m_new)
                      + jnp.dot(vt_of(h, kb), p, preferred_element_type=F32))
        return m_new

    def block(kb, slot, ms, ahead):
        bias = bias_block(kb)
        new = []
        for h in range(nheads):
            if ahead is not None:
                qk(h, ahead, 1 - slot)
            new.append(softmax_pv(h, kb, slot, ms[h], bias))
        return tuple(new)

    def two_blocks(i, ms):
        kb = 2 * i
        return block(kb + 1, 1, block(kb, 0, ms, kb + 1), kb + 2)

    acc_scr[...] = jnp.zeros(acc_scr.shape, F32)
    for h in range(nheads):
        qk(h, 0, 0)
    pairs = last // 2
    ms = lax.fori_loop(0, pairs, two_blocks, (jnp.full((1, TQ), NEG, F32),) * nheads)

    @pl.when(last % 2 == 0)
    def _():
        block(last, 0, ms, None)

    @pl.when(last % 2 == 1)
    def _():
        block(last, 1, block(last - 1, 0, ms, last), None)

    for h in range(nheads):
        acc = acc_scr[h]
        ot_scr[h * HEAD_DIM:(h + 1) * HEAD_DIM, :] = acc[:HEAD_DIM] * (1.0 / acc[HEAD_DIM:HEAD_DIM + 1])


def _ones_rows_from(vt, first_row):
    row = lax.broadcasted_iota(jnp.int32, vt.shape, 0)
    return jnp.where(row >= first_row, 1.0, vt)


def _store_transposed(o_ref, ot_scr):
    for c in range(ot_scr.shape[0] // LANES):
        o_ref[:, c * LANES:(c + 1) * LANES] = ot_scr[c * LANES:(c + 1) * LANES, :].T.astype(o_ref.dtype)


def _select_threshold(key_scr, hi_scr, lo_scr, ntiles, topk):
    i16_min, i16_max = -(2 ** 15), 2 ** 15 - 1
    slab = 16
    chains = 4

    def count_ge(scr, cand):
        c16 = cand.astype(jnp.int16)
        parts = [jnp.zeros((slab, TQ), jnp.int16)] * chains
        for i in range(ntiles):
            m = jnp.where(scr[i] >= c16, jnp.int16(1), jnp.int16(0))
            for r in range(TK // slab):
                parts[r % chains] = parts[r % chains] + m[r * slab:(r + 1) * slab]
        part = functools.reduce(lambda a, b: a + b, parts)
        return jnp.sum(part.astype(jnp.int32), axis=0, keepdims=True)

    def radix16(scr, need, n_all):
        def refine(state, cand):
            thr, n_ge = state
            c = count_ge(scr, cand)
            ok = c >= need
            return jnp.where(ok, cand, thr), jnp.where(ok, c, n_ge)

        state = refine((jnp.full((1, TQ), i16_min, jnp.int32), n_all), jnp.zeros((1, TQ), jnp.int32))
        return lax.fori_loop(
            0, 15, lambda b, st: refine(st, st[0] | jnp.left_shift(jnp.int32(1), 14 - b)), state)

    for i in range(ntiles):
        hi_scr[i] = (key_scr[i] >> 16).astype(jnp.int16)
    n_all = jnp.full((1, TQ), ntiles * TK, jnp.int32)
    t_hi, n_hi = radix16(hi_scr, topk, n_all)
    n_above = jnp.where(t_hi == i16_max, 0, count_ge(hi_scr, jnp.minimum(t_hi + 1, i16_max)))

    t_hi16 = t_hi.astype(jnp.int16)
    for i in range(ntiles):
        lo = ((key_scr[i] & 0xFFFF) + i16_min).astype(jnp.int16)
        lo_scr[i] = jnp.where(hi_scr[i] == t_hi16, lo, jnp.int16(i16_min))
    t_lo, n_lo = radix16(lo_scr, topk - n_above, n_hi - n_above)
    return jnp.left_shift(t_hi, 16) | (t_lo - i16_min), n_above + n_lo


def _dsa_kernel(qi_ref, qa_ref, kvw_ref, o_ref, key_scr, hi_scr, lo_scr, bias_scr, raw_scr, vt_scr,
                acc_scr, ot_scr, thr_scr, cnt_scr, *, topk):
    n = pl.program_id(1)
    nq = key_scr.shape[0]

    @pl.when(n == 0)
    def _():
        for kb in range(nq):
            vt = kvw_ref[kb * TK:(kb + 1) * TK, 2 * LANES:3 * LANES].astype(F32).T
            vt_scr[kb] = _ones_rows_from(vt, HEAD_DIM).astype(BF16)

    def rows(kb, c0):
        return kvw_ref[pl.ds(pl.multiple_of(kb * TK, TK), TK), c0:c0 + LANES]

    w_t = kvw_ref[pl.ds(pl.multiple_of(n * TQ, TQ), TQ), 3 * LANES:4 * LANES].astype(F32).T
    s_loc = lax.broadcasted_iota(jnp.int32, (TK, TQ), 0)
    t_loc = lax.broadcasted_iota(jnp.int32, (TK, TQ), 1)

    def score_block(kb):
        k_idx = rows(kb, LANES)
        acc = jnp.zeros((TK, TQ), F32)
        for h in range(IDX_HEADS):
            d = lax.dot_general(k_idx, qi_ref[h], _NT, preferred_element_type=F32)
            acc = acc + jnp.maximum(d, 0.0) * w_t[h:h + 1, :]
        bits = pltpu.bitcast(acc, jnp.int32)
        key = bits ^ ((bits >> 31) & jnp.int32(0x7FFFFFFF))
        admissible = (kb - n) * TK + s_loc <= t_loc
        key_scr[kb] = jnp.where(admissible, key, jnp.int32(INT_MIN))

    def four_score_blocks(i, carry):
        for r in range(4):
            score_block(4 * i + r)
        return carry

    quads = (n + 1) // 4
    lax.fori_loop(0, quads, four_score_blocks, 0)
    left = (n + 1) % 4

    @pl.when(left >= 2)
    def _():
        score_block(4 * quads)
        score_block(4 * quads + 1)

    @pl.when(left % 2 == 1)
    def _():
        score_block(n)

    for c in range(nq):
        @pl.when(n == c)
        def _(c=c):
            thr, n_ge = _select_threshold(key_scr, hi_scr, lo_scr, c + 1, topk)
            thr_scr[...] = jnp.broadcast_to(thr, thr_scr.shape)
            cnt_scr[...] = jnp.broadcast_to(n_ge, cnt_scr.shape)

    thr = thr_scr[0:1, :]
    n_ge = cnt_scr[0:1, :]
    floor = jnp.maximum(thr, jnp.int32(INT_MIN + 1))

    def bias_block(kb, carry):
        bias_scr[kb] = jnp.where(key_scr[kb] >= floor, 0.0, NEG).astype(BF16)
        return carry

    lax.fori_loop(0, n + 1, bias_block, 0)

    tied = jnp.where(thr != INT_MIN, jnp.where(n_ge > topk, 1.0, 0.0), 0.0)

    @pl.when(jnp.max(tied) > 0.0)
    def _():
        before = (lax.broadcasted_iota(jnp.int32, (TK, TK), 0)
                  > lax.broadcasted_iota(jnp.int32, (TK, TK), 1)).astype(BF16)

        def count_eq(kb, c):
            return c + jnp.sum(jnp.where(key_scr[kb] == thr, 1.0, 0.0), axis=0, keepdims=True)

        n_eq = lax.fori_loop(0, n + 1, count_eq, jnp.zeros((1, TQ), F32))
        need = topk - (n_ge.astype(F32) - n_eq)

        def rebias(kb, seen):
            key = key_scr[kb]
            eq = jnp.where(key == thr, 1.0, 0.0)
            rank = jnp.dot(before, eq.astype(BF16), preferred_element_type=F32) + seen
            keep = jnp.where(key > thr, 1.0, jnp.where(rank < need, eq, 0.0))
            bias_scr[kb] = jnp.where(tied > 0.0, jnp.where(keep > 0.0, 0.0, NEG),
                                     bias_scr[kb].astype(F32)).astype(BF16)
            return seen + jnp.sum(eq, axis=0, keepdims=True)

        lax.fori_loop(0, n + 1, rebias, jnp.zeros((1, TQ), F32))

    _attend_heads(A_HEADS, lambda h: qa_ref[h], lambda h, kb: rows(kb, 0), lambda kb: bias_scr[kb],
                  lambda h, kb: vt_scr[kb], raw_scr, acc_scr, ot_scr, n)
    _store_transposed(o_ref, ot_scr)


def _dsa(q_idx, q_a, kvw, *, batch, seq):
    nq = seq // TQ
    topk = min(TOPK_MAX, seq // 4)
    qblk = lambda nh: pl.BlockSpec((None, nh, TQ, LANES), lambda b, n: (b, 0, n, 0))
    width = A_HEADS * HEAD_DIM
    tile = lambda dtype: pltpu.VMEM((nq, TK, TQ), dtype)
    return pl.pallas_call(
        functools.partial(_dsa_kernel, topk=topk),
        grid=(batch, nq),
        in_specs=[qblk(IDX_HEADS), qblk(A_HEADS), pl.BlockSpec((seq, 4 * LANES), lambda b, n: (b, 0))],
        out_specs=pl.BlockSpec((TQ, width), lambda b, n: (b * nq + n, 0)),
        out_shape=jax.ShapeDtypeStruct((batch * seq, width), BF16),
        scratch_shapes=[tile(jnp.int32), tile(jnp.int16), tile(jnp.int16), tile(BF16),
                        pltpu.VMEM((2, A_HEADS, TK, TQ), F32), pltpu.VMEM((nq, LANES, TK), BF16),
                        pltpu.VMEM((A_HEADS, LANES, TQ), F32), pltpu.VMEM((width, TQ), F32),
                        pltpu.VMEM((8, TQ), jnp.int32), pltpu.VMEM((8, TQ), jnp.int32)],
        compiler_params=_params(("parallel", "arbitrary")),
        name="dsa_attention",
    )(q_idx, q_a, kvw)


DIL_PAIRS = 6


def _dilated_kernel(q_ref, k_ref, v_ref, bias_ref, o_ref, qm_scr, raw_scr, vt_scr, acc_scr, ot_scr):
    n = pl.program_id(2)
    nq = vt_scr.shape[1]
    nheads = 2 * DIL_PAIRS

    @pl.when(n == 0)
    def _():
        for p in range(DIL_PAIRS):
            for kb in range(nq):
                vt = v_ref[kb * TK:(kb + 1) * TK, p * LANES:(p + 1) * LANES].astype(F32).T
                vt_scr[2 * p, kb] = _ones_rows_from(vt, HEAD_DIM).astype(BF16)
                vt_scr[2 * p + 1, kb] = jnp.concatenate(
                    [vt[HEAD_DIM:], jnp.ones((LANES - HEAD_DIM, TK), F32)], axis=0).astype(BF16)

    lane = lax.broadcasted_iota(jnp.int32, (TQ, LANES), 1)
    for h in range(nheads):
        q = q_ref[:, (h // 2) * LANES:(h // 2 + 1) * LANES].astype(F32)
        qm_scr[h] = jnp.where((lane // HALF) % 2 == h % 2, q, 0.0).astype(BF16)

    def k_of(h, kb):
        return k_ref[pl.ds(pl.multiple_of(kb * TK, TK), TK), (h // 2) * LANES:(h // 2 + 1) * LANES]

    _attend_heads(nheads, lambda h: qm_scr[h], k_of, lambda kb: bias_ref[n - kb],
                  lambda h, kb: vt_scr[h, kb], raw_scr, acc_scr, ot_scr, n)
    _store_transposed(o_ref, ot_scr)


def _dilated(q_b, k_b, v_b, bias, *, batch, seq):
    nq = seq // TQ
    width = DIL_PAIRS * LANES
    nheads = 2 * DIL_PAIRS
    qblk = pl.BlockSpec((TQ, width), lambda b, g, n: (b * nq + n, g))
    kvblk = pl.BlockSpec((seq, width), lambda b, g, n: (b, g))
    return pl.pallas_call(
        _dilated_kernel,
        grid=(batch, B_PAIRS // DIL_PAIRS, nq),
        in_specs=[qblk, kvblk, kvblk, _resident(bias.shape)],
        out_specs=qblk,
        out_shape=jax.ShapeDtypeStruct(q_b.shape, BF16),
        scratch_shapes=[pltpu.VMEM((nheads, TQ, LANES), BF16), pltpu.VMEM((2, nheads, TK, TQ), F32),
                        pltpu.VMEM((nheads, nq, LANES, TK), BF16), pltpu.VMEM((nheads, LANES, TQ), F32),
                        pltpu.VMEM((nheads * HEAD_DIM, TQ), F32)],
        compiler_params=_params(("parallel", "parallel", "arbitrary")),
        name="dilated_attention",
    )(q_b, k_b, v_b, bias)


def _out_proj_kernel(x_ref, oa_ref, ob_ref, uc_ref, uprev_ref, wcol_ref, wpool_ref, pscale_ref,
                     wo_ref, o_ref, *, seq):
    tm = x_ref.shape[0]
    pos0 = (pl.program_id(0) * tm) % seq
    u = uc_ref[...].astype(F32)
    hist = jnp.where(pos0 == 0, 0.0, uprev_ref[...].astype(F32))
    ext = jnp.concatenate([hist, u], axis=0)
    s2 = ext + pltpu.roll(ext, 1, axis=0)
    s4 = s2 + pltpu.roll(s2, 2, axis=0)
    s8 = s4 + pltpu.roll(s4, 4, axis=0)
    s16 = s8 + pltpu.roll(s8, 8, axis=0)
    wcol = wcol_ref[...]
    sums = jnp.where(wcol == 2.0, s2, jnp.where(wcol == 4.0, s4, jnp.where(wcol == 8.0, s8, s16)))
    pos = (pos0 + lax.broadcasted_iota(jnp.int32, (tm, 1), 0)).astype(F32)
    y = sums[POOL_HIST:] / jnp.minimum(pos + 1.0, wcol) - u
    oc = (jnp.dot(y.astype(BF16), wpool_ref[...], preferred_element_type=F32) * pscale_ref[...]).astype(BF16)
    a_w, b_w = oa_ref.shape[1], ob_ref.shape[1]
    o_ref[...] = (x_ref[...]
                  + jnp.dot(oa_ref[...], wo_ref[:a_w], preferred_element_type=F32)
                  + jnp.dot(ob_ref[...], wo_ref[a_w:a_w + b_w], preferred_element_type=F32)
                  + jnp.dot(oc, wo_ref[a_w + b_w:], preferred_element_type=F32))


def _out_proj(x2, o_a, o_b, u_c, wcol, wpool, pscale, wo, *, layer, seq):
    n, d = x2.shape
    tm = 2 * TQ
    lres = functools.partial(_layer_resident, layer=layer)
    row = lambda a: pl.BlockSpec((tm, a.shape[1]), lambda i: (i, 0))
    hist = pl.BlockSpec((POOL_HIST, u_c.shape[1]),
                        lambda i: (jnp.maximum(i * (tm // POOL_HIST) - 1, 0), 0))
    return pl.pallas_call(
        functools.partial(_out_proj_kernel, seq=seq),
        grid=(n // tm,),
        in_specs=[row(x2), row(o_a), row(o_b), row(u_c), hist, _resident(wcol.shape),
                  lres(wpool), lres(pscale), lres(wo)],
        out_specs=row(x2),
        out_shape=jax.ShapeDtypeStruct((n, d), F32),
        compiler_params=_params(("parallel",)),
        name="out_proj",
    )(x2, o_a, o_b, u_c, u_c, wcol, wpool, pscale, wo)


def _mlp_kernel(x_ref, g_ref, wup_ref, wdn_ref, gfin_ref, o_ref, h_scr, *, final):
    j = pl.program_id(1)

    @pl.when(j == 0)
    def _():
        x = x_ref[...]
        h_scr[...] = (x * lax.rsqrt(jnp.mean(x * x, axis=-1, keepdims=True) + EPS) * g_ref[...]).astype(BF16)
        o_ref[...] = x

    a = jnp.dot(h_scr[...], wup_ref[...], preferred_element_type=F32)
    a = jnp.square(jnp.maximum(a, 0.0)).astype(BF16)
    o_ref[...] += jnp.dot(a, wdn_ref[...], preferred_element_type=F32)

    if final:
        @pl.when(j == pl.num_programs(1) - 1)
        def _():
            y = o_ref[...]
            o_ref[...] = y * lax.rsqrt(jnp.mean(y * y, axis=-1, keepdims=True) + EPS) * gfin_ref[...]


def _mlp(x2, g, wup, wdn, gfin, *, layer, tm, tf, final):
    n, d = x2.shape
    dff = wup.shape[-1]
    return pl.pallas_call(
        functools.partial(_mlp_kernel, final=final),
        grid=(n // tm, dff // tf),
        in_specs=[pl.BlockSpec((tm, d), lambda i, j: (i, 0)), _layer_resident(g, layer),
                  pl.BlockSpec((None, d, tf), lambda i, j: (layer, 0, j)),
                  pl.BlockSpec((None, tf, d), lambda i, j: (layer, j, 0)),
                  _resident(gfin.shape)],
        out_specs=pl.BlockSpec((tm, d), lambda i, j: (i, 0)),
        out_shape=jax.ShapeDtypeStruct((n, d), F32),
        scratch_shapes=[pltpu.VMEM((tm, d), BF16)],
        compiler_params=_params(("parallel", "arbitrary")),
        name="mlp",
    )(x2, g, wup, wdn, gfin)


def kernel(x, g_mix, w_in, g_cq, w_uq, w_uq_idx, w_pool, pool_scale, w_o, g_mlp, w_up, w_down, g_final):
    batch, seq, d_model = x.shape
    depth = w_in.shape[0]
    assert TQ == TK and seq % (2 * TQ) == 0
    cos, sin_signed = _rope_tables(seq)
    dil_bias = _dilated_bias(seq // TK)
    groups, gdim = w_pool.shape[1], w_pool.shape[2]
    wcol = jnp.asarray(np.repeat(np.asarray(POOL_WINDOWS, np.float32), gdim)[None, :])

    w_in_p, a_q_rank, b_width, c_width = _in_layout(w_in, d_model)
    wq_p = jnp.concatenate([_padded_heads(w_uq, A_HEADS) * (HEAD_DIM ** -0.5),
                            _padded_heads(w_uq_idx, IDX_HEADS)], axis=-1).astype(BF16)
    wpool_p = jnp.stack([jax.scipy.linalg.block_diag(*[w_pool[l, g] for g in range(groups)])
                         for l in range(depth)]).astype(BF16)
    wo_p, wup_p, wdn_p = w_o.astype(BF16), w_up.astype(BF16), w_down.astype(BF16)
    g_mix_p, g_cq_p, g_mlp_p, pscale_p = (a[:, None, :] for a in (g_mix, g_cq, g_mlp, pool_scale))

    x2 = x.reshape(batch * seq, d_model)
    for l in range(depth):
        q_a, q_idx, kvw, q_b, k_b, v_b, u_c = _in_proj(
            x2, g_mix_p, w_in_p, g_cq_p, wq_p, cos, sin_signed, layer=l,
            batch=batch, seq=seq, a_q_rank=a_q_rank, b_width=b_width, c_width=c_width)
        o_a = _dsa(q_idx, q_a, kvw, batch=batch, seq=seq)
        o_b = _dilated(q_b, k_b, v_b, dil_bias, batch=batch, seq=seq)
        x2 = _out_proj(x2, o_a, o_b, u_c, wcol, wpool_p, pscale_p, wo_p, layer=l, seq=seq)
        x2 = _mlp(x2, g_mlp_p, wup_p, wdn_p, g_final[None, :], layer=l, tm=1024, tf=512,
                  final=(l == depth - 1))
    return x2.reshape(batch, seq, d_model)
```
